```python
import jax
import jax.numpy as jnp
from jax import lax
import numpy as np


D_MODEL = 4096
BATCH = 2
SEQ = 8192
DEPTH = 1

RWKV_WIDTH = D_MODEL // 2
RWKV_HEAD = 64
RWKV_HEADS = RWKV_WIDTH // RWKV_HEAD
RET_WIDTH = D_MODEL - RWKV_WIDTH
RET_HEAD = 256
RET_HEADS = RET_WIDTH // RET_HEAD
IN_COLS = 4 * RWKV_WIDTH + 4 * RET_WIDTH
LORA_DECAY = 128
LORA_ICL = 128
RET_CHUNK = 128
ROPE_BASE = 10000.0
N_EXPERTS = 32
TOP_K = 4
D_EXPERT = 1536
SWIGLU_LIMIT = 7.0
SWIGLU_ALPHA = 1.702
MOE_BLOCK = 256
NORM_EPS = 1e-5
RWKV_GN_EPS = 64e-5
RET_GN_EPS = 1e-6

kernel_name = 'hybrid_rwkv7_retention_moe_adaln'


def rms_norm(x, w):
    xf = x.astype(jnp.float32)
    y = xf * lax.rsqrt(jnp.mean(xf * xf, axis=-1, keepdims=True) + NORM_EPS)
    return (y * w.astype(jnp.float32)).astype(x.dtype)


def token_shift(t):
    return jnp.pad(t, ((0, 0), (1, 0), (0, 0)))[:, :-1]


def rwkv7_time_mix(h, p, mu_rkvg, mu_wa, w0, w1, w2, a0, a1, a2, k_k, k_a, r_k, lnx_w, lnx_b):
    bsz, s, _ = h.shape
    f32 = jnp.float32
    p = p + (token_shift(p) - p) * mu_rkvg
    r, k, v, g = jnp.split(p, 4, axis=-1)
    dh = token_shift(h) - h
    xw = h + dh * mu_wa[0]
    xa = h + dh * mu_wa[1]
    w = -jax.nn.softplus(-(w0 + jnp.tanh(xw @ w1) @ w2)) - 0.5
    a = jax.nn.sigmoid(a0 + (xa @ a1) @ a2)

    def heads(t):
        return t.reshape(bsz, s, RWKV_HEADS, RWKV_HEAD).astype(f32)

    kk = heads(k * k_k)
    kk = kk / jnp.maximum(jnp.sqrt(jnp.sum(kk * kk, axis=-1, keepdims=True)), 1e-12)
    k = k * (1.0 + (a - 1.0) * k_a)
    rh, kh, vh, ah = heads(r), heads(k), heads(v), heads(a)
    decay = jnp.exp(-jnp.exp(heads(w)))
    a_in = -kk
    b_in = kk * ah

    def step(state, inp):
        r_t, w_t, k_t, v_t, a_t, b_t = inp
        sa = jnp.einsum('bhvk,bhk->bhv', state, a_t)
        state = state * w_t[:, :, None, :] + sa[..., None] * b_t[:, :, None, :] + v_t[..., None] * k_t[:, :, None, :]
        return state, jnp.einsum('bhvk,bhk->bhv', state, r_t)

    def seq_first(t):
        return jnp.moveaxis(t, 1, 0)

    s0 = jnp.zeros((bsz, RWKV_HEADS, RWKV_HEAD, RWKV_HEAD), f32)
    _, y = lax.scan(step, s0, (seq_first(rh), seq_first(decay), seq_first(kh), seq_first(vh), seq_first(a_in), seq_first(b_in)))
    y = jnp.moveaxis(y, 0, 1)
    mean = jnp.mean(y, axis=-1, keepdims=True)
    var = jnp.mean(jnp.square(y - mean), axis=-1, keepdims=True)
    y = (y - mean) * lax.rsqrt(var + RWKV_GN_EPS)
    y = y.reshape(bsz, s, RWKV_WIDTH) * lnx_w.astype(f32) + lnx_b.astype(f32)
    bonus = jnp.sum(rh * kh * r_k.astype(f32), axis=-1, keepdims=True) * vh
    out = (y + bonus.reshape(bsz, s, RWKV_WIDTH)) * jax.nn.sigmoid(g.astype(f32))
    return out.astype(h.dtype)


def rope(t, positions):
    half = t.shape[-1] // 2
    inv_freq = ROPE_BASE ** (-jnp.arange(half, dtype=jnp.float32) / half)
    ang = positions.astype(jnp.float32)[..., None] * inv_freq
    cos = jnp.cos(ang)[:, :, None, :]
    sin = jnp.sin(ang)[:, :, None, :]
    t1, t2 = t[..., :half], t[..., half:]
    return jnp.concatenate([t1 * cos - t2 * sin, t1 * sin + t2 * cos], axis=-1)


def retention(p, positions):
    bsz, s, _ = p.shape
    f32 = jnp.float32
    q, k, v, g = jnp.split(p, 4, axis=-1)

    def heads(t):
        return t.reshape(bsz, s, RET_HEADS, RET_HEAD).astype(f32)

    q = rope(heads(q), positions)
    k = rope(heads(k), positions) * (RET_HEAD ** -0.5)
    v = heads(v)
    log_gamma = jnp.log(1.0 - 2.0 ** (-5.0 - jnp.arange(RET_HEADS, dtype=f32)))
    idx = jnp.arange(RET_CHUNK, dtype=f32)
    diff = idx[:, None] - idx[None, :]
    inner = jnp.where(diff >= 0, jnp.exp(log_gamma[:, None, None] * jnp.maximum(diff, 0.0)), 0.0)
    q_decay = jnp.exp(log_gamma[:, None] * (idx + 1.0))[None, :, :, None]
    k_decay = jnp.exp(log_gamma[:, None] * (RET_CHUNK - 1.0 - idx))[None, :, :, None]
    chunk_decay = jnp.exp(log_gamma * RET_CHUNK)[None, :, None, None]
    n_chunks = s // RET_CHUNK

    def chunked(t):
        return t.reshape(bsz, n_chunks, RET_CHUNK, RET_HEADS, RET_HEAD).transpose(1, 0, 3, 2, 4)

    def step(state, inp):
        qc, kc, vc = inp
        scores = jnp.einsum('bhnd,bhmd->bhnm', qc, kc) * inner
        o = jnp.einsum('bhnm,bhme->bhne', scores, vc) + jnp.einsum('bhnd,bhde->bhne', qc, state) * q_decay
        state = state * chunk_decay + jnp.einsum('bhmd,bhme->bhde', kc * k_decay, vc)
        return state, o

    r0 = jnp.zeros((bsz, RET_HEADS, RET_HEAD, RET_HEAD), f32)
    _, o = lax.scan(step, r0, (chunked(q), chunked(k), chunked(v)))
    o = o.transpose(1, 0, 3, 2, 4).reshape(bsz, s, RET_HEADS, RET_HEAD)
    o = o * lax.rsqrt(jnp.mean(o * o, axis=-1, keepdims=True) + RET_GN_EPS)
    out = o.reshape(bsz, s, RET_WIDTH) * jax.nn.silu(g.astype(f32))
    return out.astype(p.dtype)


def moe_ffn(h, w_router, b_router, w_gate_up, b_gate_up, w_down, b_down):
    bsz, s, d = h.shape
    xt = h.reshape(-1, d)
    n_tok = xt.shape[0]
    n_assign = n_tok * TOP_K
    logits = (xt @ w_router + b_router).astype(jnp.float32)
    top_val, top_idx = lax.top_k(logits, TOP_K)
    gates = jax.nn.softmax(top_val, axis=-1)
    flat_e = top_idx.reshape(n_assign)
    order = jnp.argsort(flat_e)
    sorted_e = flat_e[order]
    counts = jnp.bincount(flat_e, length=N_EXPERTS)
    padded = (counts + MOE_BLOCK - 1) // MOE_BLOCK * MOE_BLOCK
    pad_end = jnp.cumsum(padded)
    pad_start = pad_end - padded
    start = jnp.cumsum(counts) - counts
    dest = pad_start[sorted_e] + jnp.arange(n_assign) - start[sorted_e]
    n_blocks = (n_assign + N_EXPERTS * (MOE_BLOCK - 1) + MOE_BLOCK - 1) // MOE_BLOCK
    cap = n_blocks * MOE_BLOCK
    row_tok = jnp.full((cap,), n_tok, jnp.int32).at[dest].set((order // TOP_K).astype(jnp.int32))
    row_gate = jnp.zeros((cap,), jnp.float32).at[dest].set(gates.reshape(n_assign)[order])
    rows = xt.at[row_tok].get(mode='fill', fill_value=0)
    block_e = jnp.minimum(jnp.searchsorted(pad_end, jnp.arange(n_blocks) * MOE_BLOCK, side='right'), N_EXPERTS - 1)

    def expert_block(args):
        xb, e = args
        gu = xb @ w_gate_up[e] + b_gate_up[e]
        gate = jnp.minimum(gu[:, :D_EXPERT], SWIGLU_LIMIT)
        up = jnp.clip(gu[:, D_EXPERT:], -SWIGLU_LIMIT, SWIGLU_LIMIT)
        act = (up + 1.0) * gate * jax.nn.sigmoid(SWIGLU_ALPHA * gate)
        return act @ w_down[e] + b_down[e]

    out_rows = lax.map(expert_block, (rows.reshape(n_blocks, MOE_BLOCK, d), block_e)).reshape(cap, d)
    weighted = out_rows * row_gate[:, None].astype(out_rows.dtype)
    y = jnp.zeros((n_tok, d), out_rows.dtype).at[row_tok].add(weighted, mode='drop')
    return y.reshape(bsz, s, d).astype(h.dtype)


def setup_inputs(seed: int = 0) -> dict:
    key = jax.random.key(seed)
    ks = jax.random.split(key, 32)
    f32 = jnp.float32
    L, D = DEPTH, D_MODEL

    def nrm(k, shape, scale):
        return scale * jax.random.normal(k, shape, f32)

    offset = jax.random.randint(ks[2], (BATCH, 1), 0, 4096)
    positions = (offset + jnp.arange(SEQ)[None, :]).astype(jnp.int32)
    decay_base = -6.0 + 5.0 * jnp.linspace(0.0, 1.0, RWKV_WIDTH, dtype=f32) ** 0.9
    return {
        'x': nrm(ks[0], (BATCH, SEQ, D), 1.0),
        'c': nrm(ks[1], (BATCH, D), 1.0),
        'positions': positions,
        'w_ada': nrm(ks[3], (L, D, 6 * D), D ** -0.5),
        'b_ada': nrm(ks[4], (L, 6 * D), 0.01),
        'norm1_w': 1.0 + nrm(ks[5], (L, D), 0.02),
        'w_in': nrm(ks[6], (L, D, IN_COLS), D ** -0.5),
        'rwkv_mu_rkvg': jax.random.uniform(ks[7], (L, 4 * RWKV_WIDTH), f32),
        'rwkv_mu_wa': jax.random.uniform(ks[8], (L, 2, D), f32),
        'rwkv_w0': decay_base[None, :] + nrm(ks[9], (L, RWKV_WIDTH), 0.1),
        'rwkv_w1': nrm(ks[10], (L, D, LORA_DECAY), D ** -0.5),
        'rwkv_w2': nrm(ks[11], (L, LORA_DECAY, RWKV_WIDTH), 0.1 * LORA_DECAY ** -0.5),
        'rwkv_a0': nrm(ks[12], (L, RWKV_WIDTH), 0.1),
        'rwkv_a1': nrm(ks[13], (L, D, LORA_ICL), D ** -0.5),
        'rwkv_a2': nrm(ks[14], (L, LORA_ICL, RWKV_WIDTH), 0.5 * LORA_ICL ** -0.5),
        'rwkv_k_k': 0.85 + nrm(ks[15], (L, RWKV_WIDTH), 0.02),
        'rwkv_k_a': 1.0 + nrm(ks[16], (L, RWKV_WIDTH), 0.02),
        'rwkv_r_k': -0.04 + nrm(ks[17], (L, RWKV_HEADS, RWKV_HEAD), 0.02),
        'rwkv_lnx_w': 1.0 + nrm(ks[18], (L, RWKV_WIDTH), 0.02),
        'rwkv_lnx_b': nrm(ks[19], (L, RWKV_WIDTH), 0.01),
        'w_out': nrm(ks[20], (L, D, D), D ** -0.5),
        'norm2_w': 1.0 + nrm(ks[21], (L, D), 0.02),
        'w_router': nrm(ks[22], (L, D, N_EXPERTS), D ** -0.5),
        'b_router': nrm(ks[23], (L, N_EXPERTS), 0.01),
        'w_gate_up': nrm(ks[24], (L, N_EXPERTS, D, 2 * D_EXPERT), D ** -0.5),
        'b_gate_up': nrm(ks[25], (L, N_EXPERTS, 2 * D_EXPERT), 0.01),
        'w_down': nrm(ks[26], (L, N_EXPERTS, D_EXPERT, D), D_EXPERT ** -0.5),
        'b_down': nrm(ks[27], (L, N_EXPERTS, D), 0.01),
        'normf_w': 1.0 + nrm(ks[28], (D,), 0.02),
    }


def reference(x, c, positions, w_ada, b_ada, norm1_w, w_in, rwkv_mu_rkvg, rwkv_mu_wa, rwkv_w0, rwkv_w1, rwkv_w2, rwkv_a0, rwkv_a1, rwkv_a2, rwkv_k_k, rwkv_k_a, rwkv_r_k, rwkv_lnx_w, rwkv_lnx_b, w_out, norm2_w, w_router, b_router, w_gate_up, b_gate_up, w_down, b_down, normf_w):
    c_act = jax.nn.silu(c)
    for l in range(DEPTH):
        mod = (c_act @ w_ada[l] + b_ada[l])[:, None, :]
        shift1, scale1, gate1, shift2, scale2, gate2 = jnp.split(mod, 6, axis=-1)
        h = rms_norm(x, norm1_w[l]) * (1.0 + scale1) + shift1
        p = h @ w_in[l]
        y_rwkv = rwkv7_time_mix(h, p[..., :4 * RWKV_WIDTH], rwkv_mu_rkvg[l], rwkv_mu_wa[l], rwkv_w0[l], rwkv_w1[l], rwkv_w2[l], rwkv_a0[l], rwkv_a1[l], rwkv_a2[l], rwkv_k_k[l], rwkv_k_a[l], rwkv_r_k[l], rwkv_lnx_w[l], rwkv_lnx_b[l])
        y_ret = retention(p[..., 4 * RWKV_WIDTH:], positions)
        mix = jnp.concatenate([y_rwkv, y_ret], axis=-1)
        x = x + gate1 * (mix @ w_out[l])
        h = rms_norm(x, norm2_w[l]) * (1.0 + scale2) + shift2
        x = x + gate2 * moe_ffn(h, w_router[l], b_router[l], w_gate_up[l], b_gate_up[l], w_down[l], b_down[l])
    return rms_norm(x, normf_w)
```

```python
import functools

import jax
import jax.numpy as jnp
from jax import lax
from jax.experimental import pallas as pl
from jax.experimental.pallas import tpu as pltpu

F32 = jnp.float32
BF16 = jnp.bfloat16

RWKV_HEAD = 64
RET_HEAD = 256
RET_CHUNK = 128
ROPE_BASE = 10000.0
TOP_K = 4
SWIGLU_LIMIT = 7.0
SWIGLU_ALPHA = 1.702
NORM_EPS = 1e-5
RWKV_GN_EPS = 64e-5
RET_GN_EPS = 1e-6

LANES = 128
RWKV_CHUNK = 64
RWKV_GROUP = 256
HEADS_PER_GROUP = RWKV_GROUP // RWKV_HEAD
MOE_ROWS = 256
VMEM_LIMIT = 56 << 20


def _cp(*sem):
    return pltpu.CompilerParams(dimension_semantics=sem, vmem_limit_bytes=VMEM_LIMIT)


def _dot(a, b):
    return jnp.dot(a, b, preferred_element_type=F32)


def _dot_nt(a, b):
    return lax.dot_general(a, b, (((1,), (1,)), ((), ())), preferred_element_type=F32)


def _dot_tn(a, b):
    return lax.dot_general(a, b, (((0,), (0,)), ((), ())), preferred_element_type=F32)


def _split3(x):
    hi = x.astype(BF16)
    r1 = x - hi.astype(F32)
    mid = r1.astype(BF16)
    lo = (r1 - mid.astype(F32)).astype(BF16)
    return hi, mid, lo


def _rms(x, w):
    return x * lax.rsqrt(jnp.mean(x * x, axis=-1, keepdims=True) + NORM_EPS) * w


def _adaln_body(c_ref, w_ref, b_ref, o_ref):
    c = c_ref[...]
    ca = c * jax.nn.sigmoid(c)
    o_ref[...] = _dot(ca.astype(BF16), w_ref[...].astype(BF16)) + b_ref[...]


def _adaln(c, w_ada, b_ada):
    bsz, d = c.shape
    n = w_ada.shape[1]
    tn = min(512, n)
    c8 = jnp.zeros((8, d), F32).at[:bsz].set(c)
    out = pl.pallas_call(
        _adaln_body,
        grid=(n // tn,),
        in_specs=[pl.BlockSpec((8, d), lambda j: (0, 0)),
                  pl.BlockSpec((d, tn), lambda j: (0, j)),
                  pl.BlockSpec((1, tn), lambda j: (0, j))],
        out_specs=pl.BlockSpec((8, tn), lambda j: (0, j)),
        out_shape=jax.ShapeDtypeStruct((8, n), F32),
        compiler_params=_cp("arbitrary"),
        name="adaln",
    )(c8, w_ada, b_ada.reshape(1, n))
    return out[:bsz]


def _norm1_body(tiles_per_seq, x_ref, nw_ref, sc_ref, sh_ref, muw_ref, mua_ref, w1_ref, a1_ref,
                h_ref, tw_ref, la_ref, carry):
    i = pl.program_id(0)

    @pl.when(i % tiles_per_seq == 0)
    def _():
        carry[...] = jnp.zeros_like(carry)

    x = x_ref[...]
    tm = x.shape[0]
    h = _rms(x, nw_ref[...]) * (1.0 + sc_ref[...]) + sh_ref[...]
    row = lax.broadcasted_iota(jnp.int32, h.shape, 0)
    h_prev = jnp.where(row == 0, carry[0:1, :], pltpu.roll(h, 1, 0))
    carry[0:1, :] = h[tm - 1:tm, :]
    dh = h_prev - h
    xw = h + dh * muw_ref[...]
    xa = h + dh * mua_ref[...]
    tw_ref[...] = jnp.tanh(_dot(xw.astype(BF16), w1_ref[...]))
    la_ref[...] = _dot(xa.astype(BF16), a1_ref[...])
    h_ref[...] = h.astype(BF16)


def _norm1_lora(x2d, nw, scale, shift, mu_w, mu_a, w1b, a1b, seq):
    t, d = x2d.shape
    tm = min(256, seq)
    tps = seq // tm
    r = w1b.shape[1]
    row_d = pl.BlockSpec((1, d), lambda i: (0, 0))
    mod = pl.BlockSpec((None, 1, d), lambda i: (i // tps, 0, 0))
    lora = pl.BlockSpec((d, r), lambda i: (0, 0))
    return pl.pallas_call(
        functools.partial(_norm1_body, tps),
        grid=(t // tm,),
        in_specs=[pl.BlockSpec((tm, d), lambda i: (i, 0)), row_d, mod, mod, row_d, row_d, lora, lora],
        out_specs=[pl.BlockSpec((tm, d), lambda i: (i, 0)),
                   pl.BlockSpec((tm, r), lambda i: (i, 0)),
                   pl.BlockSpec((tm, r), lambda i: (i, 0))],
        out_shape=[jax.ShapeDtypeStruct((t, d), BF16),
                   jax.ShapeDtypeStruct((t, r), F32),
                   jax.ShapeDtypeStruct((t, r), F32)],
        scratch_shapes=[pltpu.VMEM((8, d), F32)],
        compiler_params=_cp("arbitrary"),
        name="norm1_lora",
    )(x2d, nw, scale, shift, mu_w, mu_a, w1b, a1b)


def _matmul_body(a_ref, b_ref, o_ref):
    o_ref[...] = _dot(a_ref[...], b_ref[...])


def _matmul(a, b):
    m, k = a.shape
    n = b.shape[1]
    tm, tn = min(1024, m), min(512, n)
    return pl.pallas_call(
        _matmul_body,
        grid=(m // tm, n // tn),
        in_specs=[pl.BlockSpec((tm, k), lambda i, j: (i, 0)),
                  pl.BlockSpec((k, tn), lambda i, j: (0, j))],
        out_specs=pl.BlockSpec((tm, tn), lambda i, j: (i, j)),
        out_shape=jax.ShapeDtypeStruct((m, n), F32),
        compiler_params=_cp("parallel", "arbitrary"),
        name="in_proj",
    )(a, b)


def _rwkv_body(tb, pr_ref, pk_ref, pv_ref, pg_ref, tw_ref, la_ref,
               mur_ref, muk_ref, muv_ref, mug_ref, w0_ref, a0_ref, kk_ref, ka_ref, rk_ref,
               lnw_ref, lnb_ref, w2_ref, a2_ref,
               bd_ref, mstk_ref, sm_ref, im_ref, tri_ref, eye_ref, lvl_ref,
               y_ref,
               carry, state, r_s, lw_s, k_s, v_s, a_s, b_s, y_s):
    i = pl.program_id(2)
    c = RWKV_CHUNK

    @pl.when(i == 0)
    def _():
        carry[...] = jnp.zeros_like(carry)
        state[...] = jnp.zeros_like(state)

    row = lax.broadcasted_iota(jnp.int32, (tb, RWKV_GROUP), 0)

    def shift_mix(ref, slot, mu):
        cur = ref[...]
        prev = jnp.where(row == 0, carry[slot:slot + 1, :], pltpu.roll(cur, 1, 0))
        carry[slot:slot + 1, :] = cur[tb - 1:tb, :]
        return cur + (prev - cur) * mu

    bd = bd_ref[...]

    def seg_sum(x):
        hi, mid, lo = _split3(x)
        return _dot(hi, bd) + _dot(mid, bd) + _dot(lo, bd)

    r = shift_mix(pr_ref, 0, mur_ref[...])
    k = shift_mix(pk_ref, 1, muk_ref[...])
    v = shift_mix(pv_ref, 2, muv_ref[...])
    g = shift_mix(pg_ref, 3, mug_ref[...])

    z = -(w0_ref[...] + _dot(tw_ref[...].astype(BF16), w2_ref[...]))
    softplus = jnp.maximum(z, 0.0) + jnp.log(1.0 + jnp.exp(-jnp.abs(z)))
    lw = -jnp.exp(-softplus - 0.5)
    a_icl = jax.nn.sigmoid(a0_ref[...] + _dot(la_ref[...].astype(BF16), a2_ref[...]))
    kk = k * kk_ref[...]
    kk = kk / jnp.maximum(jnp.sqrt(seg_sum(kk * kk)), 1e-12)
    k = k * (1.0 + (a_icl - 1.0) * ka_ref[...])
    r_s[...] = r
    lw_s[...] = lw
    k_s[...] = k
    v_s[...] = v
    a_s[...] = -kk
    b_s[...] = kk * a_icl

    def chunk(ci, carry_):
        rows = pl.ds(pl.multiple_of(ci * c, c), c)
        lw_c = lw_s[rows, :]
        hi, mid, lo = _split3(lw_c)
        tri = tri_ref[...]
        cum = _dot(tri, hi) + _dot(tri, mid) + _dot(tri, lo)
        cum_end = cum[c - 1:c, :]
        p_prev = jnp.exp(cum - lw_c)
        p_cur = jnp.exp(cum)
        p_inv = jnp.exp(-cum)
        p_end = jnp.exp(cum_end - cum)
        mstk = mstk_ref[...]

        def stack(x):
            return (jnp.concatenate([x] * HEADS_PER_GROUP, axis=0) * mstk).astype(BF16)

        r_c, k_c, v_c, a_c, b_c = r_s[rows, :], k_s[rows, :], v_s[rows, :], a_s[rows, :], b_s[rows, :]
        a_t = stack(a_c * p_prev)
        r_t = stack(r_c * p_cur)
        b_t = stack(b_c * p_inv)
        k_t = stack(k_c * p_inv)
        b_h = stack(b_c * p_end)
        k_h = stack(k_c * p_end)
        v_t = stack(v_c)
        strict = sm_ref[...] > 0.0
        incl = im_ref[...] > 0.0
        a_ab = jnp.where(strict, _dot_nt(a_t, b_t), 0.0)
        a_ak = jnp.where(strict, _dot_nt(a_t, k_t), 0.0)
        a_rb = jnp.where(incl, _dot_nt(r_t, b_t), 0.0)
        a_rk = jnp.where(incl, _dot_nt(r_t, k_t), 0.0)
        inv = eye_ref[...] + a_ab * lvl_ref[0]
        for lv in range(1, c.bit_length() - 1):
            invb = inv.astype(BF16)
            w = _dot((a_ab * lvl_ref[lv]).astype(BF16), invb)
            inv = inv + _dot(invb, w.astype(BF16))
        s0 = state[...]
        s0b = s0.astype(BF16)
        x0 = _dot_nt(a_t, s0b) + _dot(a_ak.astype(BF16), v_t)
        u = _dot(inv.astype(BF16), x0.astype(BF16)).astype(BF16)
        ys = _dot_nt(r_t, s0b) + _dot(a_rb.astype(BF16), u) + _dot(a_rk.astype(BF16), v_t)
        y = ys[0:c]
        for hh in range(1, HEADS_PER_GROUP):
            y = y + ys[hh * c:(hh + 1) * c]
        y_s[rows, :] = y
        state[...] = s0 * jnp.exp(cum_end) + _dot_tn(u, b_h) + _dot_tn(v_t, k_h)
        return carry_

    lax.fori_loop(0, tb // c, chunk, 0)

    y = y_s[...]
    inv_n = 1.0 / RWKV_HEAD
    mean = seg_sum(y) * inv_n
    dlt = y - mean
    var = seg_sum(dlt * dlt) * inv_n
    yn = dlt * lax.rsqrt(var + RWKV_GN_EPS) * lnw_ref[...] + lnb_ref[...]
    bonus = seg_sum(r_s[...] * k_s[...] * rk_ref[...]) * v_s[...]
    y_ref[...] = ((yn + bonus) * jax.nn.sigmoid(g)).astype(BF16)


def _rwkv_consts():
    n = RWKV_GROUP
    c = RWKV_CHUNK
    idx = jnp.arange(n)
    head = idx // RWKV_HEAD
    blk = idx // c
    tpos = idx % c
    bd = (head[:, None] == head[None, :]).astype(BF16)
    mstk = (blk[:, None] == head[None, :]).astype(F32)
    same = blk[:, None] == blk[None, :]
    strict = (same & (tpos[None, :] < tpos[:, None])).astype(F32)
    incl = (same & (tpos[None, :] <= tpos[:, None])).astype(F32)
    tri = (jnp.arange(c)[None, :] <= jnp.arange(c)[:, None]).astype(BF16)
    eye = jnp.eye(n, dtype=F32)
    lvls = []
    for lv in range(c.bit_length() - 1):
        m = 1 << lv
        hi_lo = ((idx[:, None] // m) % 2 == 1) & ((idx[None, :] // m) % 2 == 0)
        lvls.append((hi_lo & (idx[:, None] // (2 * m) == idx[None, :] // (2 * m))).astype(F32))
    return bd, mstk, strict, incl, tri, eye, jnp.stack(lvls)


def _rwkv(p, tw, la, prm, w2b, a2b, bsz, seq, rw):
    t = p.shape[0]
    g = rw // RWKV_GROUP
    tb = min(256, seq)
    nt = seq // tb
    r = tw.shape[1]
    bd, mstk, strict, incl, tri, eye, lvls = _rwkv_consts()

    def pcol(m):
        return pl.BlockSpec((tb, RWKV_GROUP), lambda b, gg, i: (b * nt + i, m * g + gg))

    lor = pl.BlockSpec((tb, r), lambda b, gg, i: (b * nt + i, 0))
    chan = pl.BlockSpec((1, RWKV_GROUP), lambda b, gg, i: (0, gg))
    lw2 = pl.BlockSpec((r, RWKV_GROUP), lambda b, gg, i: (0, gg))
    sq = pl.BlockSpec((RWKV_GROUP, RWKV_GROUP), lambda b, gg, i: (0, 0))
    tri_spec = pl.BlockSpec((RWKV_CHUNK, RWKV_CHUNK), lambda b, gg, i: (0, 0))
    blk = pltpu.VMEM((tb, RWKV_GROUP), F32)
    return pl.pallas_call(
        functools.partial(_rwkv_body, tb),
        grid=(bsz, g, nt),
        in_specs=[pcol(0), pcol(1), pcol(2), pcol(3), lor, lor] + [chan] * 11 + [lw2, lw2]
                 + [sq, sq, sq, sq, tri_spec, sq, pl.BlockSpec(lvls.shape, lambda b, gg, i: (0, 0, 0))],
        out_specs=pl.BlockSpec((tb, RWKV_GROUP), lambda b, gg, i: (b * nt + i, gg)),
        out_shape=jax.ShapeDtypeStruct((t, rw), BF16),
        scratch_shapes=[pltpu.VMEM((8, RWKV_GROUP), F32), pltpu.VMEM((RWKV_GROUP, RWKV_GROUP), F32)]
                       + [blk] * 7,
        compiler_params=_cp("parallel", "parallel", "arbitrary"),
        name="rwkv7",
    )(p, p, p, p, tw, la, *prm, w2b, a2b, bd, mstk, strict, incl, tri, eye, lvls)


def _ret_body(heads, q_ref, k_ref, v_ref, g_ref, pos_ref, invf_ref, inner_ref, qd_ref, kd_ref, cd_ref,
              o_ref, state):
    @pl.when(pl.program_id(1) == 0)
    def _():
        state[...] = jnp.zeros_like(state)

    half = RET_HEAD // 2
    ang = pos_ref[...].astype(F32) * invf_ref[...]
    cos = jnp.cos(ang)
    sin = jnp.sin(ang)

    def rope(x):
        x1, x2 = x[:, :half], x[:, half:]
        return jnp.concatenate([x1 * cos - x2 * sin, x1 * sin + x2 * cos], axis=-1)

    for hh in range(heads):
        sl = slice(hh * RET_HEAD, (hh + 1) * RET_HEAD)
        q = rope(q_ref[:, sl])
        k = rope(k_ref[:, sl]) * (RET_HEAD ** -0.5)
        v = v_ref[:, sl].astype(BF16)
        g = g_ref[:, sl]
        qb = q.astype(BF16)
        scores = _dot_nt(qb, k.astype(BF16)) * inner_ref[hh]
        st = state[hh]
        o = _dot(scores.astype(BF16), v) + _dot(qb, st.astype(BF16)) * qd_ref[hh]
        state[hh] = st * cd_ref[hh] + _dot_tn((k * kd_ref[hh]).astype(BF16), v)
        on = o * lax.rsqrt(jnp.mean(o * o, axis=-1, keepdims=True) + RET_GN_EPS)
        o_ref[:, sl] = (on * (g * jax.nn.sigmoid(g))).astype(BF16)


def _retention(p, positions, bsz, seq, col0, width):
    t = p.shape[0]
    heads = width // RET_HEAD
    c = RET_CHUNK
    nc = seq // c
    half = RET_HEAD // 2
    cb = col0 // width
    inv_freq = (ROPE_BASE ** (-jnp.arange(half, dtype=F32) / half)).reshape(1, half)
    log_gamma = jnp.log(1.0 - 2.0 ** (-5.0 - jnp.arange(heads, dtype=F32)))
    idx = jnp.arange(c, dtype=F32)
    diff = idx[:, None] - idx[None, :]
    inner = jnp.where(diff >= 0, jnp.exp(log_gamma[:, None, None] * jnp.maximum(diff, 0.0)), 0.0)
    q_decay = jnp.exp(log_gamma[:, None] * (idx + 1.0))[:, :, None]
    k_decay = jnp.exp(log_gamma[:, None] * (c - 1.0 - idx))[:, :, None]
    chunk_decay = jnp.broadcast_to(jnp.exp(log_gamma * c)[:, None, None], (heads, 1, RET_HEAD))

    def pcol(m):
        return pl.BlockSpec((c, width), lambda b, i: (b * nc + i, cb + m))

    def full(a):
        return pl.BlockSpec(a.shape, lambda b, i: (0,) * a.ndim)

    consts = (inv_freq, inner, q_decay, k_decay, chunk_decay)
    return pl.pallas_call(
        functools.partial(_ret_body, heads),
        grid=(bsz, nc),
        in_specs=[pcol(0), pcol(1), pcol(2), pcol(3),
                  pl.BlockSpec((c, 1), lambda b, i: (b * nc + i, 0))] + [full(a) for a in consts],
        out_specs=pl.BlockSpec((c, width), lambda b, i: (b * nc + i, 0)),
        out_shape=jax.ShapeDtypeStruct((t, width), BF16),
        scratch_shapes=[pltpu.VMEM((heads, RET_HEAD, RET_HEAD), F32)],
        compiler_params=_cp("parallel", "arbitrary"),
        name="retention",
    )(p, p, p, p, positions.reshape(t, 1), *consts)


def _outproj_body(ya_ref, yb_ref, wa_ref, wb_ref, x_ref, g_ref, o_ref):
    mix = _dot(ya_ref[...], wa_ref[...]) + _dot(yb_ref[...], wb_ref[...])
    o_ref[...] = x_ref[...] + g_ref[...] * mix


def _outproj(ya, yb, wa, wb, x2d, gate, seq):
    t, d = x2d.shape
    ka, kb = ya.shape[1], yb.shape[1]
    tm, tn = min(1024, seq), min(512, d)
    tps = seq // tm
    return pl.pallas_call(
        _outproj_body,
        grid=(t // tm, d // tn),
        in_specs=[pl.BlockSpec((tm, ka), lambda i, j: (i, 0)),
                  pl.BlockSpec((tm, kb), lambda i, j: (i, 0)),
                  pl.BlockSpec((ka, tn), lambda i, j: (0, j)),
                  pl.BlockSpec((kb, tn), lambda i, j: (0, j)),
                  pl.BlockSpec((tm, tn), lambda i, j: (i, j)),
                  pl.BlockSpec((None, 1, tn), lambda i, j: (i // tps, 0, j))],
        out_specs=pl.BlockSpec((tm, tn), lambda i, j: (i, j)),
        out_shape=jax.ShapeDtypeStruct((t, d), F32),
        compiler_params=_cp("parallel", "arbitrary"),
        name="out_proj",
    )(ya, yb, wa, wb, x2d, gate)


def _router_body(x_ref, nw_ref, sc_ref, sh_ref, wr_ref, br_ref, tri_ref,
                 h_ref, idx_ref, gate_ref, rank_ref, cnt_ref, carry):
    @pl.when(pl.program_id(0) == 0)
    def _():
        carry[...] = jnp.zeros_like(carry)

    h = _rms(x_ref[...], nw_ref[...]) * (1.0 + sc_ref[...]) + sh_ref[...]
    h_ref[...] = h
    logits = jnp.dot(h, wr_ref[...], precision=lax.Precision.HIGHEST, preferred_element_type=F32) + br_ref[...]
    lane = lax.broadcasted_iota(jnp.int32, logits.shape, 1)
    vals, idxs = [], []
    rest = logits
    for _ in range(TOP_K):
        m = jnp.max(rest, axis=-1, keepdims=True)
        ix = jnp.min(jnp.where(rest == m, lane, LANES), axis=-1, keepdims=True)
        vals.append(m)
        idxs.append(ix)
        rest = jnp.where(lane == ix, -jnp.inf, rest)
    exps = [jnp.exp(vv - vals[0]) for vv in vals]
    den = exps[0]
    for e in exps[1:]:
        den = den + e
    hot = (lane == idxs[0]).astype(F32)
    for ix in idxs[1:]:
        hot = hot + (lane == ix).astype(F32)
    before = _dot(tri_ref[...], hot.astype(BF16)) + carry[0:1, :]
    carry[0:1, :] = carry[0:1, :] + jnp.sum(hot, axis=0, keepdims=True)
    idx_out = jnp.zeros(logits.shape, jnp.int32)
    gate_out = jnp.zeros(logits.shape, F32)
    rank_out = jnp.zeros(logits.shape, F32)
    for kk in range(TOP_K):
        rk = jnp.sum(jnp.where(lane == idxs[kk], before, 0.0), axis=-1, keepdims=True)
        idx_out = jnp.where(lane == kk, idxs[kk], idx_out)
        gate_out = jnp.where(lane == kk, exps[kk] / den, gate_out)
        rank_out = jnp.where(lane == kk, rk, rank_out)
    idx_ref[...] = idx_out
    gate_ref[...] = gate_out
    rank_ref[...] = rank_out.astype(jnp.int32)
    cnt_ref[...] = jnp.broadcast_to(carry[0:1, :], cnt_ref.shape).astype(jnp.int32)


def _router(x2d, nw, scale, shift, w_router, b_router, seq):
    t, d = x2d.shape
    e = w_router.shape[1]
    tm = min(256, seq)
    tps = seq // tm
    wr = jnp.zeros((d, LANES), F32).at[:, :e].set(w_router)
    br = jnp.full((1, LANES), -1e30, F32).at[0, :e].set(b_router)
    tri = (jnp.arange(tm)[None, :] < jnp.arange(tm)[:, None]).astype(BF16)
    row_d = pl.BlockSpec((1, d), lambda i: (0, 0))
    mod = pl.BlockSpec((None, 1, d), lambda i: (i // tps, 0, 0))
    lane_out = pl.BlockSpec((tm, LANES), lambda i: (i, 0))
    return pl.pallas_call(
        _router_body,
        grid=(t // tm,),
        in_specs=[pl.BlockSpec((tm, d), lambda i: (i, 0)), row_d, mod, mod,
                  pl.BlockSpec((d, LANES), lambda i: (0, 0)),
                  pl.BlockSpec((1, LANES), lambda i: (0, 0)),
                  pl.BlockSpec((tm, tm), lambda i: (0, 0))],
        out_specs=[pl.BlockSpec((tm, d), lambda i: (i, 0)), lane_out, lane_out, lane_out,
                   pl.BlockSpec((8, LANES), lambda i: (0, 0))],
        out_shape=[jax.ShapeDtypeStruct((t, d), F32),
                   jax.ShapeDtypeStruct((t, LANES), jnp.int32),
                   jax.ShapeDtypeStruct((t, LANES), F32),
                   jax.ShapeDtypeStruct((t, LANES), jnp.int32),
                   jax.ShapeDtypeStruct((8, LANES), jnp.int32)],
        scratch_shapes=[pltpu.VMEM((8, LANES), F32)],
        compiler_params=_cp("arbitrary"),
        name="norm2_router",
    )(x2d, nw, scale, shift, wr, br, tri)


def _row_copy(src, dst, sem):
    return pltpu.make_async_copy(src, dst, sem)


def _dispatch_body(tm, dest_ref, h_ref, rows_in, rows_out, sem):
    del rows_in

    def issue(r, carry_):
        for kk in range(TOP_K):
            d = dest_ref[0, r * TOP_K + kk]
            _row_copy(h_ref.at[pl.ds(r, 1)], rows_out.at[pl.ds(d, 1)], sem).start()
        return carry_

    def drain(r, carry_):
        for _ in range(TOP_K):
            _row_copy(h_ref.at[pl.ds(0, 1)], rows_out.at[pl.ds(0, 1)], sem).wait()
        return carry_

    lax.fori_loop(0, tm, issue, 0)
    lax.fori_loop(0, tm, drain, 0)


def _dispatch(h2, dest, cap):
    t, d = h2.shape
    tm = min(128, t)
    dest3 = dest.reshape(t // tm, 1, tm * TOP_K)
    return pl.pallas_call(
        functools.partial(_dispatch_body, tm),
        grid=(t // tm,),
        in_specs=[pl.BlockSpec((None, 1, tm * TOP_K), lambda i: (i, 0, 0), memory_space=pltpu.SMEM),
                  pl.BlockSpec((tm, d), lambda i: (i, 0)),
                  pl.BlockSpec(memory_space=pl.ANY)],
        out_specs=pl.BlockSpec(memory_space=pl.ANY),
        out_shape=jax.ShapeDtypeStruct((cap, d), F32),
        scratch_shapes=[pltpu.SemaphoreType.DMA],
        input_output_aliases={2: 0},
        compiler_params=_cp("arbitrary"),
        name="moe_dispatch",
    )(dest3, h2, jnp.zeros((cap, d), F32))


def _expert_body(be_ref, na_ref, x_ref, wg_ref, wu_ref, bg_ref, bu_ref, wd_ref, bd_ref, o_ref, xb):
    i = pl.program_id(0)
    j = pl.program_id(1)

    @pl.when(i >= na_ref[0])
    def _():
        o_ref[...] = jnp.zeros_like(o_ref)

    @pl.when(i < na_ref[0])
    def _():
        @pl.when(j == 0)
        def _():
            xb[...] = x_ref[...].astype(BF16)

        x = xb[...]
        gate = jnp.minimum(_dot(x, wg_ref[...]) + bg_ref[...], SWIGLU_LIMIT)
        up = jnp.clip(_dot(x, wu_ref[...]) + bu_ref[...], -SWIGLU_LIMIT, SWIGLU_LIMIT)
        act = (up + 1.0) * gate * jax.nn.sigmoid(SWIGLU_ALPHA * gate)
        part = _dot(act.astype(BF16), wd_ref[...])

        @pl.when(j == 0)
        def _():
            o_ref[...] = part + bd_ref[...]

        @pl.when(j > 0)
        def _():
            o_ref[...] = o_ref[...] + part


def _experts(rows, block_e, n_active, wgu, bgu, wd, bd):
    cap, d = rows.shape
    e, _, two_de = wgu.shape
    de = two_de // 2
    bm = MOE_ROWS
    tn = min(256, de)
    nj = de // tn
    nb = cap // bm

    def blk(i, na):
        return jnp.minimum(i, na[0] - 1)

    grid_spec = pltpu.PrefetchScalarGridSpec(
        num_scalar_prefetch=2,
        grid=(nb, nj),
        in_specs=[pl.BlockSpec((bm, d), lambda i, j, be, na: (blk(i, na), 0)),
                  pl.BlockSpec((None, d, tn), lambda i, j, be, na: (be[blk(i, na)], 0, j)),
                  pl.BlockSpec((None, d, tn), lambda i, j, be, na: (be[blk(i, na)], 0, nj + j)),
                  pl.BlockSpec((None, 1, tn), lambda i, j, be, na: (be[blk(i, na)], 0, j)),
                  pl.BlockSpec((None, 1, tn), lambda i, j, be, na: (be[blk(i, na)], 0, nj + j)),
                  pl.BlockSpec((None, tn, d), lambda i, j, be, na: (be[blk(i, na)], j, 0)),
                  pl.BlockSpec((None, 1, d), lambda i, j, be, na: (be[blk(i, na)], 0, 0))],
        out_specs=pl.BlockSpec((bm, d), lambda i, j, be, na: (i, 0)),
        scratch_shapes=[pltpu.VMEM((bm, d), BF16)],
    )
    return pl.pallas_call(
        _expert_body,
        grid_spec=grid_spec,
        out_shape=jax.ShapeDtypeStruct((cap, d), F32),
        compiler_params=_cp("arbitrary", "arbitrary"),
        name="moe_experts",
    )(block_e, n_active, rows, wgu, wgu, bgu.reshape(e, 1, two_de), bgu.reshape(e, 1, two_de),
      wd, bd.reshape(e, 1, d))


def _combine_body(tm, dest_ref, rows_ref, gates_ref, x_ref, g2_ref, nw_ref, o_ref, buf, sem):
    def issue(r, carry_):
        for kk in range(TOP_K):
            d = dest_ref[0, r * TOP_K + kk]
            _row_copy(rows_ref.at[pl.ds(d, 1)], buf.at[kk, pl.ds(r, 1)], sem).start()
        return carry_

    def drain(r, carry_):
        for kk in range(TOP_K):
            _row_copy(rows_ref.at[pl.ds(0, 1)], buf.at[kk, pl.ds(0, 1)], sem).wait()
        return carry_

    lax.fori_loop(0, tm, issue, 0)
    lax.fori_loop(0, tm, drain, 0)
    gates = gates_ref[...]
    y = gates[:, 0:1] * buf[0]
    for kk in range(1, TOP_K):
        y = y + gates[:, kk:kk + 1] * buf[kk]
    x2 = x_ref[...] + g2_ref[...] * y
    o_ref[...] = _rms(x2, nw_ref[...])


def _combine(out_rows, dest, gates, x1, gate2, normf_w, seq):
    t, d = x1.shape
    tm = min(64, seq)
    tps = seq // tm
    dest3 = dest.reshape(t // tm, 1, tm * TOP_K)
    return pl.pallas_call(
        functools.partial(_combine_body, tm),
        grid=(t // tm,),
        in_specs=[pl.BlockSpec((None, 1, tm * TOP_K), lambda i: (i, 0, 0), memory_space=pltpu.SMEM),
                  pl.BlockSpec(memory_space=pl.ANY),
                  pl.BlockSpec((tm, LANES), lambda i: (i, 0)),
                  pl.BlockSpec((tm, d), lambda i: (i, 0)),
                  pl.BlockSpec((None, 1, d), lambda i: (i // tps, 0, 0)),
                  pl.BlockSpec((1, d), lambda i: (0, 0))],
        out_specs=pl.BlockSpec((tm, d), lambda i: (i, 0)),
        out_shape=jax.ShapeDtypeStruct((t, d), F32),
        scratch_shapes=[pltpu.VMEM((TOP_K, tm, d), F32), pltpu.SemaphoreType.DMA],
        compiler_params=_cp("arbitrary"),
        name="moe_combine",
    )(dest3, out_rows, gates, x1, gate2, normf_w.reshape(1, d))


def _moe_plan(idx, rank, counts, n_experts):
    bm = MOE_ROWS
    n_assign = idx.shape[0] * TOP_K
    padded = (counts + bm - 1) // bm * bm
    pad_end = jnp.cumsum(padded)
    pad_start = pad_end - padded
    dest = (pad_start[idx] + rank).astype(jnp.int32)
    nb = (n_assign + n_experts * (bm - 1) + bm - 1) // bm
    block_e = jnp.minimum(jnp.searchsorted(pad_end, jnp.arange(nb) * bm, side='right'), n_experts - 1)
    n_active = (pad_end[-1] // bm).astype(jnp.int32).reshape(1)
    return dest, block_e.astype(jnp.int32), n_active, nb * bm


def kernel(x, c, positions, w_ada, b_ada, norm1_w, w_in, rwkv_mu_rkvg, rwkv_mu_wa, rwkv_w0, rwkv_w1, rwkv_w2, rwkv_a0, rwkv_a1, rwkv_a2, rwkv_k_k, rwkv_k_a, rwkv_r_k, rwkv_lnx_w, rwkv_lnx_b, w_out, norm2_w, w_router, b_router, w_gate_up, b_gate_up, w_down, b_down, normf_w):
    bsz, seq, d = x.shape
    t = bsz * seq
    depth = w_ada.shape[0]
    rw = rwkv_w0.shape[1]
    ret_w = d - rw
    n_experts = w_router.shape[2]
    x2d = x.reshape(t, d)
    for l in range(depth):
        mod = _adaln(c, w_ada[l], b_ada[l])
        shift1, scale1, gate1, shift2, scale2, gate2 = [m.reshape(bsz, 1, d) for m in jnp.split(mod, 6, axis=-1)]
        h, tw, la = _norm1_lora(x2d, norm1_w[l].reshape(1, d), scale1, shift1,
                                rwkv_mu_wa[l, 0].reshape(1, d), rwkv_mu_wa[l, 1].reshape(1, d),
                                rwkv_w1[l].astype(BF16), rwkv_a1[l].astype(BF16), seq)
        p = _matmul(h, w_in[l].astype(BF16))
        mu = rwkv_mu_rkvg[l].reshape(4, 1, rw)
        prm = (mu[0], mu[1], mu[2], mu[3], rwkv_w0[l].reshape(1, rw), rwkv_a0[l].reshape(1, rw),
               rwkv_k_k[l].reshape(1, rw), rwkv_k_a[l].reshape(1, rw), rwkv_r_k[l].reshape(1, rw),
               rwkv_lnx_w[l].reshape(1, rw), rwkv_lnx_b[l].reshape(1, rw))
        y_rwkv = _rwkv(p, tw, la, prm, rwkv_w2[l].astype(BF16), rwkv_a2[l].astype(BF16), bsz, seq, rw)
        y_ret = _retention(p, positions, bsz, seq, 4 * rw, ret_w)
        wo = w_out[l].astype(BF16)
        x2d = _outproj(y_rwkv, y_ret, wo[:rw], wo[rw:], x2d, gate1, seq)
        h2, idx, gates, rank, counts = _router(x2d, norm2_w[l].reshape(1, d), scale2, shift2,
                                               w_router[l], b_router[l], seq)
        dest, block_e, n_active, cap = _moe_plan(idx[:, :TOP_K], rank[:, :TOP_K], counts[0, :n_experts], n_experts)
        rows = _dispatch(h2, dest, cap)
        out_rows = _experts(rows, block_e, n_active, w_gate_up[l].astype(BF16), b_gate_up[l],
                            w_down[l].astype(BF16), b_down[l])
        if l + 1 < depth:
            raise NotImplementedError("the final norm is fused into the last layer's combine")
        out = _combine(out_rows, dest, gates, x2d, gate2, normf_w, seq)
    return out.reshape(bsz, seq, d)
```

```python
import functools

import jax
import jax.numpy as jnp
from jax import lax
from jax.experimental import pallas as pl
from jax.experimental.pallas import tpu as pltpu

F32 = jnp.float32
BF16 = jnp.bfloat16

RWKV_HEAD = 64
RET_HEAD = 256
RET_CHUNK = 128
ROPE_BASE = 10000.0
TOP_K = 4
SWIGLU_LIMIT = 7.0
SWIGLU_ALPHA = 1.702
NORM_EPS = 1e-5
RWKV_GN_EPS = 64e-5
RET_GN_EPS = 1e-6

LANES = 128
RWKV_CHUNK = 64
RWKV_GROUP = 256
HEADS_PER_GROUP = RWKV_GROUP // RWKV_HEAD
MOE_ROWS = 512
MOE_UP_TILE = 256
MOE_DOWN_TILE = 1024
VMEM_LIMIT = 56 << 20


def _cp(*sem):
    return pltpu.CompilerParams(dimension_semantics=sem, vmem_limit_bytes=VMEM_LIMIT)


def _dot(a, b):
    return jnp.dot(a, b, preferred_element_type=F32)


def _dot_nt(a, b):
    return lax.dot_general(a, b, (((1,), (1,)), ((), ())), preferred_element_type=F32)


def _dot_tn(a, b):
    return lax.dot_general(a, b, (((0,), (0,)), ((), ())), preferred_element_type=F32)


def _split3(x):
    hi = x.astype(BF16)
    r1 = x - hi.astype(F32)
    mid = r1.astype(BF16)
    lo = (r1 - mid.astype(F32)).astype(BF16)
    return hi, mid, lo


def _rms(x, w):
    return x * lax.rsqrt(jnp.mean(x * x, axis=-1, keepdims=True) + NORM_EPS) * w


def _adaln_body(c_ref, w_ref, b_ref, o_ref):
    c = c_ref[...]
    ca = c * jax.nn.sigmoid(c)
    o_ref[...] = _dot(ca.astype(BF16), w_ref[...].astype(BF16)) + b_ref[...]


def _adaln(c, w_ada, b_ada):
    bsz, d = c.shape
    n = w_ada.shape[1]
    tn = min(512, n)
    c8 = jnp.zeros((8, d), F32).at[:bsz].set(c)
    out = pl.pallas_call(
        _adaln_body,
        grid=(n // tn,),
        in_specs=[pl.BlockSpec((8, d), lambda j: (0, 0)),
                  pl.BlockSpec((d, tn), lambda j: (0, j)),
                  pl.BlockSpec((1, tn), lambda j: (0, j))],
        out_specs=pl.BlockSpec((8, tn), lambda j: (0, j)),
        out_shape=jax.ShapeDtypeStruct((8, n), F32),
        compiler_params=_cp("arbitrary"),
        name="adaln",
    )(c8, w_ada, b_ada.reshape(1, n))
    return out[:bsz]


def _norm1_body(tiles_per_seq, x_ref, nw_ref, sc_ref, sh_ref, muw_ref, mua_ref, w1_ref, a1_ref,
                h_ref, tw_ref, la_ref, carry):
    i = pl.program_id(0)

    @pl.when(i % tiles_per_seq == 0)
    def _():
        carry[...] = jnp.zeros_like(carry)

    x = x_ref[...]
    tm = x.shape[0]
    h = _rms(x, nw_ref[...]) * (1.0 + sc_ref[...]) + sh_ref[...]
    row = lax.broadcasted_iota(jnp.int32, h.shape, 0)
    h_prev = jnp.where(row == 0, carry[0:1, :], pltpu.roll(h, 1, 0))
    carry[0:1, :] = h[tm - 1:tm, :]
    dh = h_prev - h
    xw = h + dh * muw_ref[...]
    xa = h + dh * mua_ref[...]
    tw_ref[...] = jnp.tanh(_dot(xw.astype(BF16), w1_ref[...]))
    la_ref[...] = _dot(xa.astype(BF16), a1_ref[...])
    h_ref[...] = h.astype(BF16)


def _norm1_lora(x2d, nw, scale, shift, mu_w, mu_a, w1b, a1b, seq):
    t, d = x2d.shape
    tm = min(256, seq)
    tps = seq // tm
    r = w1b.shape[1]
    row_d = pl.BlockSpec((1, d), lambda i: (0, 0))
    mod = pl.BlockSpec((None, 1, d), lambda i: (i // tps, 0, 0))
    lora = pl.BlockSpec((d, r), lambda i: (0, 0))
    return pl.pallas_call(
        functools.partial(_norm1_body, tps),
        grid=(t // tm,),
        in_specs=[pl.BlockSpec((tm, d), lambda i: (i, 0)), row_d, mod, mod, row_d, row_d, lora, lora],
        out_specs=[pl.BlockSpec((tm, d), lambda i: (i, 0)),
                   pl.BlockSpec((tm, r), lambda i: (i, 0)),
                   pl.BlockSpec((tm, r), lambda i: (i, 0))],
        out_shape=[jax.ShapeDtypeStruct((t, d), BF16),
                   jax.ShapeDtypeStruct((t, r), F32),
                   jax.ShapeDtypeStruct((t, r), F32)],
        scratch_shapes=[pltpu.VMEM((8, d), F32)],
        compiler_params=_cp("arbitrary"),
        name="norm1_lora",
    )(x2d, nw, scale, shift, mu_w, mu_a, w1b, a1b)


def _matmul_body(a_ref, b_ref, o_ref):
    o_ref[...] = _dot(a_ref[...], b_ref[...])


def _matmul(a, b):
    m, k = a.shape
    n = b.shape[1]
    tm, tn = min(1024, m), min(512, n)
    return pl.pallas_call(
        _matmul_body,
        grid=(m // tm, n // tn),
        in_specs=[pl.BlockSpec((tm, k), lambda i, j: (i, 0)),
                  pl.BlockSpec((k, tn), lambda i, j: (0, j))],
        out_specs=pl.BlockSpec((tm, tn), lambda i, j: (i, j)),
        out_shape=jax.ShapeDtypeStruct((m, n), F32),
        compiler_params=_cp("parallel", "arbitrary"),
        name="in_proj",
    )(a, b)


def _rwkv_body(tb, pr_ref, pk_ref, pv_ref, pg_ref, tw_ref, la_ref,
               mur_ref, muk_ref, muv_ref, mug_ref, w0_ref, a0_ref, kk_ref, ka_ref, rk_ref,
               lnw_ref, lnb_ref, w2_ref, a2_ref,
               bd_ref, mstk_ref, sm_ref, im_ref, tri_ref, eye_ref, lvl_ref,
               y_ref,
               carry, state):
    i = pl.program_id(2)
    c = RWKV_CHUNK

    @pl.when(i == 0)
    def _():
        carry[...] = jnp.zeros_like(carry)
        state[...] = jnp.zeros_like(state)

    row = lax.broadcasted_iota(jnp.int32, (tb, RWKV_GROUP), 0)

    def shift_mix(ref, slot, mu):
        cur = ref[...]
        prev = jnp.where(row == 0, carry[slot:slot + 1, :], pltpu.roll(cur, 1, 0))
        carry[slot:slot + 1, :] = cur[tb - 1:tb, :]
        return cur + (prev - cur) * mu

    bd = bd_ref[...]

    def seg_sum(x):
        hi, mid, lo = _split3(x)
        return _dot(hi, bd) + _dot(mid, bd) + _dot(lo, bd)

    r = shift_mix(pr_ref, 0, mur_ref[...])
    k = shift_mix(pk_ref, 1, muk_ref[...])
    v = shift_mix(pv_ref, 2, muv_ref[...])
    g = shift_mix(pg_ref, 3, mug_ref[...])

    z = -(w0_ref[...] + _dot(tw_ref[...].astype(BF16), w2_ref[...]))
    softplus = jnp.maximum(z, 0.0) + jnp.log(1.0 + jnp.exp(-jnp.abs(z)))
    lw = -jnp.exp(-softplus - 0.5)
    a_icl = jax.nn.sigmoid(a0_ref[...] + _dot(la_ref[...].astype(BF16), a2_ref[...]))
    kk = k * kk_ref[...]
    kk = kk / jnp.maximum(jnp.sqrt(seg_sum(kk * kk)), 1e-12)
    k = k * (1.0 + (a_icl - 1.0) * ka_ref[...])
    a_in = -kk
    b_in = kk * a_icl

    mstk = mstk_ref[...]
    strict = sm_ref[...] > 0.0
    incl = im_ref[...] > 0.0
    tri = tri_ref[...]

    def stack(x):
        return (jnp.concatenate([x] * HEADS_PER_GROUP, axis=0) * mstk).astype(BF16)

    def unstack(xs):
        out = xs[0:c]
        for hh in range(1, HEADS_PER_GROUP):
            out = out + xs[hh * c:(hh + 1) * c]
        return out

    n_ch = tb // c

    def each(fn, *cols):
        return [fn(*args) for args in zip(*cols)]

    lw_c = [lw[ci * c:(ci + 1) * c] for ci in range(n_ch)]

    def cumsum(x):
        hi, mid, lo = _split3(x)
        return _dot(tri, hi) + _dot(tri, mid) + _dot(tri, lo)

    cum = each(cumsum, lw_c)
    cum_end = [x[c - 1:c, :] for x in cum]
    p_inv = each(lambda x: jnp.exp(-x), cum)
    p_end = each(lambda x, e: jnp.exp(e - x), cum, cum_end)
    rows = [slice(ci * c, (ci + 1) * c) for ci in range(n_ch)]
    a_t = each(lambda sl, x, l: stack(a_in[sl] * jnp.exp(x - l)), rows, cum, lw_c)
    r_t = each(lambda sl, x: stack(r[sl] * jnp.exp(x)), rows, cum)
    b_t = each(lambda sl, p: stack(b_in[sl] * p), rows, p_inv)
    k_t = each(lambda sl, p: stack(k[sl] * p), rows, p_inv)
    b_h = each(lambda sl, p: stack(b_in[sl] * p), rows, p_end)
    k_h = each(lambda sl, p: stack(k[sl] * p), rows, p_end)
    v_t = each(lambda sl: stack(v[sl]), rows)
    a_ab = each(lambda x, y: jnp.where(strict, _dot_nt(x, y), 0.0), a_t, b_t)
    a_ak = each(lambda x, y: jnp.where(strict, _dot_nt(x, y), 0.0).astype(BF16), a_t, k_t)
    a_rb = each(lambda x, y: jnp.where(incl, _dot_nt(x, y), 0.0).astype(BF16), r_t, b_t)
    a_rk = each(lambda x, y: jnp.where(incl, _dot_nt(x, y), 0.0).astype(BF16), r_t, k_t)
    inv = each(lambda x: eye_ref[...] + x * lvl_ref[0], a_ab)
    for lv in range(1, c.bit_length() - 1):
        invb = each(lambda x: x.astype(BF16), inv)
        w = each(lambda x, t: _dot((x * lvl_ref[lv]).astype(BF16), t).astype(BF16), a_ab, invb)
        inv = each(lambda x, t, ww: x + _dot(t, ww), inv, invb, w)
    invb = each(lambda x: x.astype(BF16), inv)
    a_p = each(lambda t, x: _dot(t, x).astype(BF16), invb, a_t)
    akv = each(lambda x, y: _dot(x, y).astype(BF16), a_ak, v_t)
    u_loc = each(lambda t, x: _dot(t, x).astype(BF16), invb, akv)
    r_p = each(lambda x, m, y: unstack(x.astype(F32) + _dot(m, y)).astype(BF16), r_t, a_rb, a_p)
    y_loc = each(lambda m, x, n, y: unstack(_dot(m, x) + _dot(n, y)), a_rb, u_loc, a_rk, v_t)
    phi = each(lambda x, y: _dot_tn(x, y).astype(BF16), a_p, b_h)
    s_loc = each(lambda x, y, z, w_: _dot_tn(x, y) + _dot_tn(z, w_), u_loc, b_h, v_t, k_h)

    s = state[...]
    ys = []
    for ci in range(n_ch):
        sb = s.astype(BF16)
        ys.append(_dot_nt(r_p[ci], sb) + y_loc[ci])
        s = s * jnp.exp(cum_end[ci]) + _dot(sb, phi[ci]) + s_loc[ci]
    state[...] = s

    y = jnp.concatenate(ys, axis=0)
    inv_n = 1.0 / RWKV_HEAD
    mean = seg_sum(y) * inv_n
    dlt = y - mean
    var = seg_sum(dlt * dlt) * inv_n
    yn = dlt * lax.rsqrt(var + RWKV_GN_EPS) * lnw_ref[...] + lnb_ref[...]
    bonus = seg_sum(r * k * rk_ref[...]) * v
    y_ref[...] = ((yn + bonus) * jax.nn.sigmoid(g)).astype(BF16)


def _rwkv_consts():
    n = RWKV_GROUP
    c = RWKV_CHUNK
    idx = jnp.arange(n)
    head = idx // RWKV_HEAD
    blk = idx // c
    tpos = idx % c
    bd = (head[:, None] == head[None, :]).astype(BF16)
    mstk = (blk[:, None] == head[None, :]).astype(F32)
    same = blk[:, None] == blk[None, :]
    strict = (same & (tpos[None, :] < tpos[:, None])).astype(F32)
    incl = (same & (tpos[None, :] <= tpos[:, None])).astype(F32)
    tri = (jnp.arange(c)[None, :] <= jnp.arange(c)[:, None]).astype(BF16)
    eye = jnp.eye(n, dtype=F32)
    lvls = []
    for lv in range(c.bit_length() - 1):
        m = 1 << lv
        hi_lo = ((idx[:, None] // m) % 2 == 1) & ((idx[None, :] // m) % 2 == 0)
        lvls.append((hi_lo & (idx[:, None] // (2 * m) == idx[None, :] // (2 * m))).astype(F32))
    return bd, mstk, strict, incl, tri, eye, jnp.stack(lvls)


def _rwkv(p, tw, la, prm, w2b, a2b, bsz, seq, rw):
    t = p.shape[0]
    g = rw // RWKV_GROUP
    tb = min(512, seq)
    nt = seq // tb
    r = tw.shape[1]
    bd, mstk, strict, incl, tri, eye, lvls = _rwkv_consts()

    def pcol(m):
        return pl.BlockSpec((tb, RWKV_GROUP), lambda b, gg, i: (b * nt + i, m * g + gg))

    lor = pl.BlockSpec((tb, r), lambda b, gg, i: (b * nt + i, 0))
    chan = pl.BlockSpec((1, RWKV_GROUP), lambda b, gg, i: (0, gg))
    lw2 = pl.BlockSpec((r, RWKV_GROUP), lambda b, gg, i: (0, gg))
    sq = pl.BlockSpec((RWKV_GROUP, RWKV_GROUP), lambda b, gg, i: (0, 0))
    tri_spec = pl.BlockSpec((RWKV_CHUNK, RWKV_CHUNK), lambda b, gg, i: (0, 0))
    return pl.pallas_call(
        functools.partial(_rwkv_body, tb),
        grid=(bsz, g, nt),
        in_specs=[pcol(0), pcol(1), pcol(2), pcol(3), lor, lor] + [chan] * 11 + [lw2, lw2]
                 + [sq, sq, sq, sq, tri_spec, sq, pl.BlockSpec(lvls.shape, lambda b, gg, i: (0, 0, 0))],
        out_specs=pl.BlockSpec((tb, RWKV_GROUP), lambda b, gg, i: (b * nt + i, gg)),
        out_shape=jax.ShapeDtypeStruct((t, rw), BF16),
        scratch_shapes=[pltpu.VMEM((8, RWKV_GROUP), F32), pltpu.VMEM((RWKV_GROUP, RWKV_GROUP), F32)],
        compiler_params=_cp("parallel", "parallel", "arbitrary"),
        name="rwkv7",
    )(p, p, p, p, tw, la, *prm, w2b, a2b, bd, mstk, strict, incl, tri, eye, lvls)


def _ret_body(heads, q_ref, k_ref, v_ref, g_ref, pos_ref, invf_ref, inner_ref, qd_ref, kd_ref, cd_ref,
              o_ref, state):
    @pl.when(pl.program_id(1) == 0)
    def _():
        state[...] = jnp.zeros_like(state)

    half = RET_HEAD // 2
    ang = pos_ref[...].astype(F32) * invf_ref[...]
    cos = jnp.cos(ang)
    sin = jnp.sin(ang)

    def rope(x):
        x1, x2 = x[:, :half], x[:, half:]
        return jnp.concatenate([x1 * cos - x2 * sin, x1 * sin + x2 * cos], axis=-1)

    for hh in range(heads):
        sl = slice(hh * RET_HEAD, (hh + 1) * RET_HEAD)
        q = rope(q_ref[:, sl])
        k = rope(k_ref[:, sl]) * (RET_HEAD ** -0.5)
        v = v_ref[:, sl].astype(BF16)
        g = g_ref[:, sl]
        qb = q.astype(BF16)
        scores = _dot_nt(qb, k.astype(BF16)) * inner_ref[hh]
        st = state[hh]
        o = _dot(scores.astype(BF16), v) + _dot(qb, st.astype(BF16)) * qd_ref[hh]
        state[hh] = st * cd_ref[hh] + _dot_tn((k * kd_ref[hh]).astype(BF16), v)
        on = o * lax.rsqrt(jnp.mean(o * o, axis=-1, keepdims=True) + RET_GN_EPS)
        o_ref[:, sl] = (on * (g * jax.nn.sigmoid(g))).astype(BF16)


def _retention(p, positions, bsz, seq, col0, width):
    t = p.shape[0]
    heads = width // RET_HEAD
    c = RET_CHUNK
    nc = seq // c
    half = RET_HEAD // 2
    cb = col0 // width
    inv_freq = (ROPE_BASE ** (-jnp.arange(half, dtype=F32) / half)).reshape(1, half)
    log_gamma = jnp.log(1.0 - 2.0 ** (-5.0 - jnp.arange(heads, dtype=F32)))
    idx = jnp.arange(c, dtype=F32)
    diff = idx[:, None] - idx[None, :]
    inner = jnp.where(diff >= 0, jnp.exp(log_gamma[:, None, None] * jnp.maximum(diff, 0.0)), 0.0)
    q_decay = jnp.exp(log_gamma[:, None] * (idx + 1.0))[:, :, None]
    k_decay = jnp.exp(log_gamma[:, None] * (c - 1.0 - idx))[:, :, None]
    chunk_decay = jnp.broadcast_to(jnp.exp(log_gamma * c)[:, None, None], (heads, 1, RET_HEAD))

    def pcol(m):
        return pl.BlockSpec((c, width), lambda b, i: (b * nc + i, cb + m))

    def full(a):
        return pl.BlockSpec(a.shape, lambda b, i: (0,) * a.ndim)

    consts = (inv_freq, inner, q_decay, k_decay, chunk_decay)
    return pl.pallas_call(
        functools.partial(_ret_body, heads),
        grid=(bsz, nc),
        in_specs=[pcol(0), pcol(1), pcol(2), pcol(3),
                  pl.BlockSpec((c, 1), lambda b, i: (b * nc + i, 0))] + [full(a) for a in consts],
        out_specs=pl.BlockSpec((c, width), lambda b, i: (b * nc + i, 0)),
        out_shape=jax.ShapeDtypeStruct((t, width), BF16),
        scratch_shapes=[pltpu.VMEM((heads, RET_HEAD, RET_HEAD), F32)],
        compiler_params=_cp("parallel", "arbitrary"),
        name="retention",
    )(p, p, p, p, positions.reshape(t, 1), *consts)


def _outproj_body(ya_ref, yb_ref, wa_ref, wb_ref, x_ref, g_ref, o_ref):
    mix = _dot(ya_ref[...], wa_ref[...]) + _dot(yb_ref[...], wb_ref[...])
    o_ref[...] = x_ref[...] + g_ref[...] * mix


def _outproj(ya, yb, wa, wb, x2d, gate, seq):
    t, d = x2d.shape
    ka, kb = ya.shape[1], yb.shape[1]
    tm, tn = min(1024, seq), min(512, d)
    tps = seq // tm
    return pl.pallas_call(
        _outproj_body,
        grid=(t // tm, d // tn),
        in_specs=[pl.BlockSpec((tm, ka), lambda i, j: (i, 0)),
                  pl.BlockSpec((tm, kb), lambda i, j: (i, 0)),
                  pl.BlockSpec((ka, tn), lambda i, j: (0, j)),
                  pl.BlockSpec((kb, tn), lambda i, j: (0, j)),
                  pl.BlockSpec((tm, tn), lambda i, j: (i, j)),
                  pl.BlockSpec((None, 1, tn), lambda i, j: (i // tps, 0, j))],
        out_specs=pl.BlockSpec((tm, tn), lambda i, j: (i, j)),
        out_shape=jax.ShapeDtypeStruct((t, d), F32),
        compiler_params=_cp("parallel", "arbitrary"),
        name="out_proj",
    )(ya, yb, wa, wb, x2d, gate)


def _pack_bf16_pairs(h):
    half = h.shape[1] // 2
    hi = lax.bitcast_convert_type(h[:, :half].astype(BF16).astype(F32), jnp.int32)
    lo = lax.bitcast_convert_type(h[:, half:].astype(BF16).astype(F32), jnp.int32)
    return hi | lax.shift_right_logical(lo, 16)


def _unpack_bf16_pairs(w):
    hi = lax.bitcast_convert_type(w & -65536, F32).astype(BF16)
    lo = lax.bitcast_convert_type(w << 16, F32).astype(BF16)
    return hi, lo


def _router_body(x_ref, nw_ref, sc_ref, sh_ref, wr_ref, br_ref, tri_ref,
                 h_ref, idx_ref, gate_ref, rank_ref, cnt_ref, carry):
    @pl.when(pl.program_id(0) == 0)
    def _():
        carry[...] = jnp.zeros_like(carry)

    h = _rms(x_ref[...], nw_ref[...]) * (1.0 + sc_ref[...]) + sh_ref[...]
    h_ref[...] = _pack_bf16_pairs(h)
    logits = jnp.dot(h, wr_ref[...], precision=lax.Precision.HIGHEST, preferred_element_type=F32) + br_ref[...]
    lane = lax.broadcasted_iota(jnp.int32, logits.shape, 1)
    vals, idxs = [], []
    rest = logits
    for _ in range(TOP_K):
        m = jnp.max(rest, axis=-1, keepdims=True)
        ix = jnp.min(jnp.where(rest == m, lane, LANES), axis=-1, keepdims=True)
        vals.append(m)
        idxs.append(ix)
        rest = jnp.where(lane == ix, -jnp.inf, rest)
    exps = [jnp.exp(vv - vals[0]) for vv in vals]
    den = exps[0]
    for e in exps[1:]:
        den = den + e
    hot = (lane == idxs[0]).astype(F32)
    for ix in idxs[1:]:
        hot = hot + (lane == ix).astype(F32)
    before = _dot(tri_ref[...], hot.astype(BF16)) + carry[0:1, :]
    carry[0:1, :] = carry[0:1, :] + jnp.sum(hot, axis=0, keepdims=True)
    idx_out = jnp.zeros(logits.shape, jnp.int32)
    gate_out = jnp.zeros(logits.shape, F32)
    rank_out = jnp.zeros(logits.shape, F32)
    for kk in range(TOP_K):
        rk = jnp.sum(jnp.where(lane == idxs[kk], before, 0.0), axis=-1, keepdims=True)
        idx_out = jnp.where(lane == kk, idxs[kk], idx_out)
        gate_out = jnp.where(lane == kk, exps[kk] / den, gate_out)
        rank_out = jnp.where(lane == kk, rk, rank_out)
    idx_ref[...] = idx_out
    gate_ref[...] = gate_out
    rank_ref[...] = rank_out.astype(jnp.int32)
    cnt_ref[...] = jnp.broadcast_to(carry[0:1, :], cnt_ref.shape).astype(jnp.int32)


def _router(x2d, nw, scale, shift, w_router, b_router, seq):
    t, d = x2d.shape
    e = w_router.shape[1]
    tm = min(256, seq)
    tps = seq // tm
    wr = jnp.zeros((d, LANES), F32).at[:, :e].set(w_router)
    br = jnp.full((1, LANES), -1e30, F32).at[0, :e].set(b_router)
    tri = (jnp.arange(tm)[None, :] < jnp.arange(tm)[:, None]).astype(BF16)
    row_d = pl.BlockSpec((1, d), lambda i: (0, 0))
    mod = pl.BlockSpec((None, 1, d), lambda i: (i // tps, 0, 0))
    lane_out = pl.BlockSpec((tm, LANES), lambda i: (i, 0))
    return pl.pallas_call(
        _router_body,
        grid=(t // tm,),
        in_specs=[pl.BlockSpec((tm, d), lambda i: (i, 0)), row_d, mod, mod,
                  pl.BlockSpec((d, LANES), lambda i: (0, 0)),
                  pl.BlockSpec((1, LANES), lambda i: (0, 0)),
                  pl.BlockSpec((tm, tm), lambda i: (0, 0))],
        out_specs=[pl.BlockSpec((tm, d // 2), lambda i: (i, 0)), lane_out, lane_out, lane_out,
                   pl.BlockSpec((8, LANES), lambda i: (0, 0))],
        out_shape=[jax.ShapeDtypeStruct((t, d // 2), jnp.int32),
                   jax.ShapeDtypeStruct((t, LANES), jnp.int32),
                   jax.ShapeDtypeStruct((t, LANES), F32),
                   jax.ShapeDtypeStruct((t, LANES), jnp.int32),
                   jax.ShapeDtypeStruct((8, LANES), jnp.int32)],
        scratch_shapes=[pltpu.VMEM((8, LANES), F32)],
        compiler_params=_cp("arbitrary"),
        name="norm2_router",
    )(x2d, nw, scale, shift, wr, br, tri)


def _row_copy(src, dst, sem):
    return pltpu.make_async_copy(src, dst, sem)


def _dispatch_body(tm, dest_ref, h_ref, rows_in, rows_out, sem):
    del rows_in

    def issue(r, carry_):
        for kk in range(TOP_K):
            d = dest_ref[0, r * TOP_K + kk]
            _row_copy(h_ref.at[pl.ds(r, 1)], rows_out.at[pl.ds(d, 1)], sem).start()
        return carry_

    def drain(r, carry_):
        for _ in range(TOP_K):
            _row_copy(h_ref.at[pl.ds(0, 1)], rows_out.at[pl.ds(0, 1)], sem).wait()
        return carry_

    lax.fori_loop(0, tm, issue, 0)
    lax.fori_loop(0, tm, drain, 0)


def _dispatch(h2, dest, cap):
    t, d = h2.shape
    tm = min(128, t)
    dest3 = dest.reshape(t // tm, 1, tm * TOP_K)
    return pl.pallas_call(
        functools.partial(_dispatch_body, tm),
        grid=(t // tm,),
        in_specs=[pl.BlockSpec((None, 1, tm * TOP_K), lambda i: (i, 0, 0), memory_space=pltpu.SMEM),
                  pl.BlockSpec((tm, d), lambda i: (i, 0)),
                  pl.BlockSpec(memory_space=pl.ANY)],
        out_specs=pl.BlockSpec(memory_space=pl.ANY),
        out_shape=jax.ShapeDtypeStruct((cap, d), h2.dtype),
        scratch_shapes=[pltpu.SemaphoreType.DMA],
        input_output_aliases={2: 0},
        compiler_params=_cp("arbitrary"),
        name="moe_dispatch",
    )(dest3, h2, jnp.zeros((cap, d), h2.dtype))


def _moe_steps(blk_start, nblk, n_active, n_tiles, nb):
    n_experts = blk_start.shape[0]
    s = jnp.arange(n_tiles * nb, dtype=jnp.int32)
    step_end = n_tiles * (blk_start + nblk)
    e = jnp.minimum(jnp.sum(step_end[None, :] <= s[:, None], axis=1), n_experts - 1).astype(jnp.int32)
    local = s - n_tiles * blk_start[e]
    per = jnp.maximum(nblk[e], 1)
    tile = local // per
    rb = blk_start[e] + local % per
    n_steps = (n_tiles * n_active).astype(jnp.int32)
    active = s < n_steps
    last = n_steps - 1
    spare = s - n_steps
    cols = (jnp.where(active, e, e[last]), jnp.where(active, tile, tile[last]), jnp.where(active, rb, rb[last]),
            jnp.where(active, tile, spare % n_tiles), jnp.where(active, rb, n_active + spare // n_tiles),
            (local % per == 0))
    return tuple(x.astype(jnp.int32) for x in cols) + (n_steps.reshape(1),)


def _gate_up_body(se, sti, sri, sto, sro, first, n_steps, x_ref, wg_ref, wu_ref, bg_ref, bu_ref,
                  act_ref, wgb, wub):
    s = pl.program_id(0)

    @pl.when(s >= n_steps[0])
    def _():
        act_ref[...] = jnp.zeros_like(act_ref)

    @pl.when(s < n_steps[0])
    def _():
        @pl.when(first[s] == 1)
        def _():
            wgb[...] = wg_ref[...].astype(BF16)
            wub[...] = wu_ref[...].astype(BF16)

        xa, xb = _unpack_bf16_pairs(x_ref[...])
        half = xa.shape[1]

        def proj(w, b_ref):
            return _dot(xa, w[:half, :]) + _dot(xb, w[half:, :]) + b_ref[...]

        gate = jnp.minimum(proj(wgb, bg_ref), SWIGLU_LIMIT)
        up = jnp.clip(proj(wub, bu_ref), -SWIGLU_LIMIT, SWIGLU_LIMIT)
        act_ref[...] = ((up + 1.0) * gate * jax.nn.sigmoid(SWIGLU_ALPHA * gate)).astype(BF16)


def _gate_up(rows, steps, wgu, bgu):
    cap, half = rows.shape
    d = 2 * half
    e, _, two_de = wgu.shape
    de = two_de // 2
    bm = MOE_ROWS
    tn = min(MOE_UP_TILE, de)
    nj = de // tn
    grid_spec = pltpu.PrefetchScalarGridSpec(
        num_scalar_prefetch=7,
        grid=(nj * (cap // bm),),
        in_specs=[pl.BlockSpec((bm, half), lambda s, se, sti, sri, *_: (sri[s], 0)),
                  pl.BlockSpec((None, d, tn), lambda s, se, sti, *_: (se[s], 0, sti[s])),
                  pl.BlockSpec((None, d, tn), lambda s, se, sti, *_: (se[s], 0, nj + sti[s])),
                  pl.BlockSpec((None, 1, tn), lambda s, se, sti, *_: (se[s], 0, sti[s])),
                  pl.BlockSpec((None, 1, tn), lambda s, se, sti, *_: (se[s], 0, nj + sti[s]))],
        out_specs=pl.BlockSpec((bm, tn), lambda s, se, sti, sri, sto, sro, *_: (sro[s], sto[s])),
        scratch_shapes=[pltpu.VMEM((d, tn), BF16), pltpu.VMEM((d, tn), BF16)],
    )
    bgu3 = bgu.reshape(e, 1, two_de)
    return pl.pallas_call(
        _gate_up_body,
        grid_spec=grid_spec,
        out_shape=jax.ShapeDtypeStruct((cap, de), BF16),
        compiler_params=_cp("arbitrary"),
        name="moe_gate_up",
    )(*steps, rows, wgu, wgu, bgu3, bgu3)


def _down_body(se, sti, sri, sto, sro, first, n_steps, a_ref, wd_ref, bd_ref, o_ref, wdb):
    s = pl.program_id(0)

    @pl.when(s >= n_steps[0])
    def _():
        o_ref[...] = jnp.zeros_like(o_ref)

    @pl.when(s < n_steps[0])
    def _():
        @pl.when(first[s] == 1)
        def _():
            wdb[...] = wd_ref[...].astype(BF16)

        o_ref[...] = _dot(a_ref[...], wdb[...]) + bd_ref[...]


def _down(act, steps, wd, bd, tn):
    cap, de = act.shape
    e, _, d = wd.shape
    bm = MOE_ROWS
    grid_spec = pltpu.PrefetchScalarGridSpec(
        num_scalar_prefetch=7,
        grid=((d // tn) * (cap // bm),),
        in_specs=[pl.BlockSpec((bm, de), lambda s, se, sti, sri, *_: (sri[s], 0)),
                  pl.BlockSpec((None, de, tn), lambda s, se, sti, *_: (se[s], 0, sti[s])),
                  pl.BlockSpec((None, 1, tn), lambda s, se, sti, *_: (se[s], 0, sti[s]))],
        out_specs=pl.BlockSpec((bm, tn), lambda s, se, sti, sri, sto, sro, *_: (sro[s], sto[s])),
        scratch_shapes=[pltpu.VMEM((de, tn), BF16)],
    )
    return pl.pallas_call(
        _down_body,
        grid_spec=grid_spec,
        out_shape=jax.ShapeDtypeStruct((cap, d), F32),
        compiler_params=_cp("arbitrary"),
        name="moe_down",
    )(*steps, act, wd, bd.reshape(e, 1, d))


def _combine_body(tm, dest_ref, rows_ref, gates_ref, x_ref, g2_ref, nw_ref, o_ref, buf, sem):
    def issue(r, carry_):
        for kk in range(TOP_K):
            d = dest_ref[0, r * TOP_K + kk]
            _row_copy(rows_ref.at[pl.ds(d, 1)], buf.at[kk, pl.ds(r, 1)], sem).start()
        return carry_

    def drain(r, carry_):
        for kk in range(TOP_K):
            _row_copy(rows_ref.at[pl.ds(0, 1)], buf.at[kk, pl.ds(0, 1)], sem).wait()
        return carry_

    lax.fori_loop(0, tm, issue, 0)
    lax.fori_loop(0, tm, drain, 0)
    gates = gates_ref[...]
    y = gates[:, 0:1] * buf[0]
    for kk in range(1, TOP_K):
        y = y + gates[:, kk:kk + 1] * buf[kk]
    x2 = x_ref[...] + g2_ref[...] * y
    o_ref[...] = _rms(x2, nw_ref[...])


def _combine(out_rows, dest, gates, x1, gate2, normf_w, seq):
    t, d = x1.shape
    tm = min(64, seq)
    tps = seq // tm
    dest3 = dest.reshape(t // tm, 1, tm * TOP_K)
    return pl.pallas_call(
        functools.partial(_combine_body, tm),
        grid=(t // tm,),
        in_specs=[pl.BlockSpec((None, 1, tm * TOP_K), lambda i: (i, 0, 0), memory_space=pltpu.SMEM),
                  pl.BlockSpec(memory_space=pl.ANY),
                  pl.BlockSpec((tm, LANES), lambda i: (i, 0)),
                  pl.BlockSpec((tm, d), lambda i: (i, 0)),
                  pl.BlockSpec((None, 1, d), lambda i: (i // tps, 0, 0)),
                  pl.BlockSpec((1, d), lambda i: (0, 0))],
        out_specs=pl.BlockSpec((tm, d), lambda i: (i, 0)),
        out_shape=jax.ShapeDtypeStruct((t, d), F32),
        scratch_shapes=[pltpu.VMEM((TOP_K, tm, d), F32), pltpu.SemaphoreType.DMA],
        compiler_params=_cp("arbitrary"),
        name="moe_combine",
    )(dest3, out_rows, gates, x1, gate2, normf_w.reshape(1, d))


def _moe_plan(idx, rank, counts, n_experts):
    bm = MOE_ROWS
    n_assign = idx.shape[0] * TOP_K
    nblk = (counts + bm - 1) // bm
    blk_end = jnp.cumsum(nblk)
    blk_start = blk_end - nblk
    dest = (bm * blk_start[idx] + rank).astype(jnp.int32)
    nb = (n_assign + n_experts * (bm - 1) + bm - 1) // bm
    return dest, blk_start.astype(jnp.int32), nblk.astype(jnp.int32), blk_end[-1].astype(jnp.int32), nb


def kernel(x, c, positions, w_ada, b_ada, norm1_w, w_in, rwkv_mu_rkvg, rwkv_mu_wa, rwkv_w0, rwkv_w1, rwkv_w2, rwkv_a0, rwkv_a1, rwkv_a2, rwkv_k_k, rwkv_k_a, rwkv_r_k, rwkv_lnx_w, rwkv_lnx_b, w_out, norm2_w, w_router, b_router, w_gate_up, b_gate_up, w_down, b_down, normf_w):
    bsz, seq, d = x.shape
    t = bsz * seq
    depth = w_ada.shape[0]
    rw = rwkv_w0.shape[1]
    ret_w = d - rw
    n_experts = w_router.shape[2]
    x2d = x.reshape(t, d)
    for l in range(depth):
        mod = _adaln(c, w_ada[l], b_ada[l])
        shift1, scale1, gate1, shift2, scale2, gate2 = [m.reshape(bsz, 1, d) for m in jnp.split(mod, 6, axis=-1)]
        h, tw, la = _norm1_lora(x2d, norm1_w[l].reshape(1, d), scale1, shift1,
                                rwkv_mu_wa[l, 0].reshape(1, d), rwkv_mu_wa[l, 1].reshape(1, d),
                                rwkv_w1[l].astype(BF16), rwkv_a1[l].astype(BF16), seq)
        p = _matmul(h, w_in[l].astype(BF16))
        mu = rwkv_mu_rkvg[l].reshape(4, 1, rw)
        prm = (mu[0], mu[1], mu[2], mu[3], rwkv_w0[l].reshape(1, rw), rwkv_a0[l].reshape(1, rw),
               rwkv_k_k[l].reshape(1, rw), rwkv_k_a[l].reshape(1, rw), rwkv_r_k[l].reshape(1, rw),
               rwkv_lnx_w[l].reshape(1, rw), rwkv_lnx_b[l].reshape(1, rw))
        y_rwkv = _rwkv(p, tw, la, prm, rwkv_w2[l].astype(BF16), rwkv_a2[l].astype(BF16), bsz, seq, rw)
        y_ret = _retention(p, positions, bsz, seq, 4 * rw, ret_w)
        wo = w_out[l].astype(BF16)
        x2d = _outproj(y_rwkv, y_ret, wo[:rw], wo[rw:], x2d, gate1, seq)
        h2, idx, gates, rank, counts = _router(x2d, norm2_w[l].reshape(1, d), scale2, shift2,
                                               w_router[l], b_router[l], seq)
        dest, blk_start, nblk, n_active, nb = _moe_plan(idx[:, :TOP_K], rank[:, :TOP_K], counts[0, :n_experts], n_experts)
        rows = _dispatch(h2, dest, nb * MOE_ROWS)
        de = w_down.shape[2]
        up_tiles = de // min(MOE_UP_TILE, de)
        act = _gate_up(rows, _moe_steps(blk_start, nblk, n_active, up_tiles, nb), w_gate_up[l], b_gate_up[l])
        tn_down = min(MOE_DOWN_TILE, d)
        out_rows = _down(act, _moe_steps(blk_start, nblk, n_active, d // tn_down, nb), w_down[l], b_down[l], tn_down)
        if l + 1 < depth:
            raise NotImplementedError("the final norm is fused into the last layer's combine")
        out = _combine(out_rows, dest, gates, x2d, gate2, normf_w, seq)
    return out.reshape(bsz, seq, d)
```

```python
import functools

import jax
import jax.numpy as jnp
from jax import lax
from jax.experimental import pallas as pl
from jax.experimental.pallas import tpu as pltpu

F32 = jnp.float32
BF16 = jnp.bfloat16

RWKV_HEAD = 64
RET_HEAD = 256
RET_CHUNK = 128
ROPE_BASE = 10000.0
TOP_K = 4
SWIGLU_LIMIT = 7.0
SWIGLU_ALPHA = 1.702
NORM_EPS = 1e-5
RWKV_GN_EPS = 64e-5
RET_GN_EPS = 1e-6

LANES = 128
RWKV_CHUNK = 64
RWKV_GROUP = 256
HEADS_PER_GROUP = RWKV_GROUP // RWKV_HEAD
MOE_ROWS = 512
MOE_UP_TILE = 256
MOE_DOWN_TILE = 1024
VMEM_LIMIT = 56 << 20


def _cp(*sem):
    return pltpu.CompilerParams(dimension_semantics=sem, vmem_limit_bytes=VMEM_LIMIT)


def _dot(a, b):
    return jnp.dot(a, b, preferred_element_type=F32)


def _dot_nt(a, b):
    return lax.dot_general(a, b, (((1,), (1,)), ((), ())), preferred_element_type=F32)


def _dot_tn(a, b):
    return lax.dot_general(a, b, (((0,), (0,)), ((), ())), preferred_element_type=F32)


def _split2(x):
    hi = x.astype(BF16)
    lo = (x - hi.astype(F32)).astype(BF16)
    return hi, lo


def _rms(x, w):
    return x * lax.rsqrt(jnp.mean(x * x, axis=-1, keepdims=True) + NORM_EPS) * w


def _adaln_body(c_ref, w_ref, b_ref, o_ref):
    c = c_ref[...]
    ca = c * jax.nn.sigmoid(c)
    o_ref[...] = _dot(ca.astype(BF16), w_ref[...].astype(BF16)) + b_ref[...]


def _adaln(c, w_ada, b_ada):
    bsz, d = c.shape
    n = w_ada.shape[1]
    tn = min(512, n)
    c8 = jnp.zeros((8, d), F32).at[:bsz].set(c)
    out = pl.pallas_call(
        _adaln_body,
        grid=(n // tn,),
        in_specs=[pl.BlockSpec((8, d), lambda j: (0, 0)),
                  pl.BlockSpec((d, tn), lambda j: (0, j)),
                  pl.BlockSpec((1, tn), lambda j: (0, j))],
        out_specs=pl.BlockSpec((8, tn), lambda j: (0, j)),
        out_shape=jax.ShapeDtypeStruct((8, n), F32),
        compiler_params=_cp("arbitrary"),
        name="adaln",
    )(c8, w_ada, b_ada.reshape(1, n))
    return out[:bsz]


def _norm1_body(tiles_per_seq, x_ref, nw_ref, sc_ref, sh_ref, muw_ref, mua_ref, w1_ref, a1_ref,
                h_ref, tw_ref, la_ref, carry):
    i = pl.program_id(0)

    @pl.when(i % tiles_per_seq == 0)
    def _():
        carry[...] = jnp.zeros_like(carry)

    x = x_ref[...]
    tm = x.shape[0]
    h = _rms(x, nw_ref[...]) * (1.0 + sc_ref[...]) + sh_ref[...]
    row = lax.broadcasted_iota(jnp.int32, h.shape, 0)
    h_prev = jnp.where(row == 0, carry[0:1, :], pltpu.roll(h, 1, 0))
    carry[0:1, :] = h[tm - 1:tm, :]
    dh = h_prev - h
    xw = h + dh * muw_ref[...]
    xa = h + dh * mua_ref[...]
    tw_ref[...] = jnp.tanh(_dot(xw.astype(BF16), w1_ref[...]))
    la_ref[...] = _dot(xa.astype(BF16), a1_ref[...])
    h_ref[...] = h.astype(BF16)


def _norm1_lora(x2d, nw, scale, shift, mu_w, mu_a, w1b, a1b, seq):
    t, d = x2d.shape
    tm = min(256, seq)
    tps = seq // tm
    r = w1b.shape[1]
    row_d = pl.BlockSpec((1, d), lambda i: (0, 0))
    mod = pl.BlockSpec((None, 1, d), lambda i: (i // tps, 0, 0))
    lora = pl.BlockSpec((d, r), lambda i: (0, 0))
    return pl.pallas_call(
        functools.partial(_norm1_body, tps),
        grid=(t // tm,),
        in_specs=[pl.BlockSpec((tm, d), lambda i: (i, 0)), row_d, mod, mod, row_d, row_d, lora, lora],
        out_specs=[pl.BlockSpec((tm, d), lambda i: (i, 0)),
                   pl.BlockSpec((tm, r), lambda i: (i, 0)),
                   pl.BlockSpec((tm, r), lambda i: (i, 0))],
        out_shape=[jax.ShapeDtypeStruct((t, d), BF16),
                   jax.ShapeDtypeStruct((t, r), F32),
                   jax.ShapeDtypeStruct((t, r), F32)],
        scratch_shapes=[pltpu.VMEM((8, d), F32)],
        compiler_params=_cp("arbitrary"),
        name="norm1_lora",
    )(x2d, nw, scale, shift, mu_w, mu_a, w1b, a1b)


def _matmul_body(a_ref, b_ref, o_ref):
    o_ref[...] = _dot(a_ref[...], b_ref[...])


def _matmul(a, b):
    m, k = a.shape
    n = b.shape[1]
    tm, tn = min(1024, m), min(512, n)
    return pl.pallas_call(
        _matmul_body,
        grid=(m // tm, n // tn),
        in_specs=[pl.BlockSpec((tm, k), lambda i, j: (i, 0)),
                  pl.BlockSpec((k, tn), lambda i, j: (0, j))],
        out_specs=pl.BlockSpec((tm, tn), lambda i, j: (i, j)),
        out_shape=jax.ShapeDtypeStruct((m, n), F32),
        compiler_params=_cp("parallel", "arbitrary"),
        name="in_proj",
    )(a, b)


def _rwkv_body(tb, pr_ref, pk_ref, pv_ref, pg_ref, tw_ref, la_ref,
               mur_ref, muk_ref, muv_ref, mug_ref, w0_ref, a0_ref, kk_ref, ka_ref, rk_ref,
               lnw_ref, lnb_ref, w2_ref, a2_ref,
               bd_ref, mstk_ref, sm_ref, im_ref, tri_ref, eye_ref, lvl_ref,
               y_ref,
               carry, state):
    i = pl.program_id(2)
    c = RWKV_CHUNK

    @pl.when(i == 0)
    def _():
        carry[...] = jnp.zeros_like(carry)
        state[...] = jnp.zeros_like(state)

    row = lax.broadcasted_iota(jnp.int32, (tb, RWKV_GROUP), 0)

    def shift_mix(ref, slot, mu):
        cur = ref[...]
        prev = jnp.where(row == 0, carry[slot:slot + 1, :], pltpu.roll(cur, 1, 0))
        carry[slot:slot + 1, :] = cur[tb - 1:tb, :]
        return cur + (prev - cur) * mu

    bd = bd_ref[...]

    def seg_sum(x):
        hi, lo = _split2(x)
        return _dot(hi, bd) + _dot(lo, bd)

    r = shift_mix(pr_ref, 0, mur_ref[...])
    k = shift_mix(pk_ref, 1, muk_ref[...])
    v = shift_mix(pv_ref, 2, muv_ref[...])
    g = shift_mix(pg_ref, 3, mug_ref[...])

    z = -(w0_ref[...] + _dot(tw_ref[...].astype(BF16), w2_ref[...]))
    softplus = jnp.maximum(z, 0.0) + jnp.log(1.0 + jnp.exp(-jnp.abs(z)))
    lw = -jnp.exp(-softplus - 0.5)
    a_icl = jax.nn.sigmoid(a0_ref[...] + _dot(la_ref[...].astype(BF16), a2_ref[...]))
    kk = k * kk_ref[...]
    kk = kk / jnp.maximum(jnp.sqrt(seg_sum(kk * kk)), 1e-12)
    k = k * (1.0 + (a_icl - 1.0) * ka_ref[...])
    a_in = -kk
    b_in = kk * a_icl

    mstk = mstk_ref[...]
    strict = sm_ref[...] > 0.0
    incl = im_ref[...] > 0.0
    same_head = bd > 0
    tri = tri_ref[...]

    def stack(x):
        return jnp.concatenate([x.astype(BF16)] * HEADS_PER_GROUP, axis=0) * mstk

    n_ch = tb // c

    def each(fn, *cols):
        return [fn(*args) for args in zip(*cols)]

    rows = [slice(ci * c, (ci + 1) * c) for ci in range(n_ch)]
    lw_c = [lw[sl] for sl in rows]

    def cumsum(x):
        hi, lo = _split2(x)
        return _dot(tri, hi) + _dot(tri, lo)

    cum = each(cumsum, lw_c)
    cum_end = [x[c - 1:c, :] for x in cum]
    p_inv = each(lambda x: jnp.exp(-x), cum)
    p_end = each(lambda x, e: jnp.exp(e - x), cum, cum_end)
    a_d = each(lambda sl, x, l: (a_in[sl] * jnp.exp(x - l)).astype(BF16), rows, cum, lw_c)
    r_d = each(lambda sl, x: r[sl] * jnp.exp(x), rows, cum)
    r_b = each(lambda x: x.astype(BF16), r_d)
    a_t = each(stack, a_d)
    b_t = each(lambda sl, p: stack(b_in[sl] * p), rows, p_inv)
    k_t = each(lambda sl, p: stack(k[sl] * p), rows, p_inv)
    v_t = each(lambda sl: stack(v[sl]), rows)
    b_h = each(lambda sl, p: (b_in[sl] * p).astype(BF16), rows, p_end)
    uk_r = each(lambda sl, p: (k[sl] * p).astype(BF16), rows, p_end)
    a_ab = each(lambda x, y: jnp.where(strict, _dot_nt(x, y), 0.0), a_d, b_t)
    a_ak = each(lambda x, y: jnp.where(strict, _dot_nt(x, y), 0.0).astype(BF16), a_d, k_t)
    a_rb = each(lambda x, y: jnp.where(incl, _dot_nt(x, y), 0.0).astype(BF16), r_b, b_t)
    a_rk = each(lambda x, y: jnp.where(incl, _dot_nt(x, y), 0.0).astype(BF16), r_b, k_t)
    inv = each(lambda x: eye_ref[...] + x * lvl_ref[0], a_ab)
    for lv in range(1, c.bit_length() - 1):
        inv_bd = each(stack, inv)
        w = each(lambda x, t: _dot((x * lvl_ref[lv]).astype(BF16), t), a_ab, inv_bd)
        inv = each(lambda x, ww: x + _dot(x.astype(BF16), stack(ww)), inv, w)
    invb = each(lambda x: x.astype(BF16), inv)
    a_p = each(_dot, invb, a_t)
    akv = each(_dot, a_ak, v_t)
    u_loc = each(lambda t, x: _dot(t, stack(x)), invb, akv)
    r_p = each(lambda x, m, y: (x + _dot(m, stack(y))).astype(BF16), r_d, a_rb, a_p)
    y_loc = each(lambda m, x, n, y: _dot(m, stack(x)) + _dot(n, y), a_rb, u_loc, a_rk, v_t)
    phi = each(lambda x, y: jnp.where(same_head, _dot_tn(x.astype(BF16), y), 0.0).astype(BF16), a_p, b_h)
    s_loc = each(lambda x, y, sl, z: jnp.where(
        same_head,
        _dot_tn(jnp.concatenate([x.astype(BF16), v[sl].astype(BF16)], axis=0), jnp.concatenate([y, z], axis=0)),
        0.0), u_loc, b_h, rows, uk_r)

    s = state[...]
    ys = []
    for ci in range(n_ch):
        sb = s.astype(BF16)
        ys.append(_dot_nt(r_p[ci], sb) + y_loc[ci])
        s = s * jnp.exp(cum_end[ci]) + _dot(sb, phi[ci]) + s_loc[ci]
    state[...] = s

    y = jnp.concatenate(ys, axis=0)
    inv_n = 1.0 / RWKV_HEAD
    mean = seg_sum(y) * inv_n
    dlt = y - mean
    var = seg_sum(dlt * dlt) * inv_n
    yn = dlt * lax.rsqrt(var + RWKV_GN_EPS) * lnw_ref[...] + lnb_ref[...]
    bonus = seg_sum(r * k * rk_ref[...]) * v
    y_ref[...] = ((yn + bonus) * jax.nn.sigmoid(g)).astype(BF16)


def _rwkv_consts():
    n = RWKV_GROUP
    c = RWKV_CHUNK
    lane = jnp.arange(n)
    head = lane // RWKV_HEAD
    blk = lane // c
    bd = (head[:, None] == head[None, :]).astype(BF16)
    mstk = (blk[:, None] == head[None, :]).astype(BF16)
    t_idx = jnp.arange(c)[:, None]
    s_idx = (lane % c)[None, :]
    strict = (s_idx < t_idx).astype(F32)
    incl = (s_idx <= t_idx).astype(F32)
    eye = (s_idx == t_idx).astype(F32)
    tri = (jnp.arange(c)[None, :] <= jnp.arange(c)[:, None]).astype(BF16)
    lvls = []
    for lv in range(c.bit_length() - 1):
        m = 1 << lv
        hi_lo = ((t_idx // m) % 2 == 1) & ((s_idx // m) % 2 == 0)
        lvls.append((hi_lo & (t_idx // (2 * m) == s_idx // (2 * m))).astype(F32))
    return bd, mstk, strict, incl, tri, eye, jnp.stack(lvls)


def _rwkv(p, tw, la, prm, w2b, a2b, bsz, seq, rw):
    t = p.shape[0]
    g = rw // RWKV_GROUP
    tb = min(512, seq)
    nt = seq // tb
    r = tw.shape[1]
    bd, mstk, strict, incl, tri, eye, lvls = _rwkv_consts()

    def pcol(m):
        return pl.BlockSpec((tb, RWKV_GROUP), lambda b, gg, i: (b * nt + i, m * g + gg))

    lor = pl.BlockSpec((tb, r), lambda b, gg, i: (b * nt + i, 0))
    chan = pl.BlockSpec((1, RWKV_GROUP), lambda b, gg, i: (0, gg))
    lw2 = pl.BlockSpec((r, RWKV_GROUP), lambda b, gg, i: (0, gg))
    sq = pl.BlockSpec((RWKV_GROUP, RWKV_GROUP), lambda b, gg, i: (0, 0))
    tri_spec = pl.BlockSpec((RWKV_CHUNK, RWKV_CHUNK), lambda b, gg, i: (0, 0))
    wide = pl.BlockSpec((RWKV_CHUNK, RWKV_GROUP), lambda b, gg, i: (0, 0))
    return pl.pallas_call(
        functools.partial(_rwkv_body, tb),
        grid=(bsz, g, nt),
        in_specs=[pcol(0), pcol(1), pcol(2), pcol(3), lor, lor] + [chan] * 11 + [lw2, lw2]
                 + [sq, sq, wide, wide, tri_spec, wide, pl.BlockSpec(lvls.shape, lambda b, gg, i: (0, 0, 0))],
        out_specs=pl.BlockSpec((tb, RWKV_GROUP), lambda b, gg, i: (b * nt + i, gg)),
        out_shape=jax.ShapeDtypeStruct((t, rw), BF16),
        scratch_shapes=[pltpu.VMEM((8, RWKV_GROUP), F32), pltpu.VMEM((RWKV_GROUP, RWKV_GROUP), F32)],
        compiler_params=_cp("parallel", "parallel", "arbitrary"),
        name="rwkv7",
    )(p, p, p, p, tw, la, *prm, w2b, a2b, bd, mstk, strict, incl, tri, eye, lvls)


def _ret_body(heads, q_ref, k_ref, v_ref, g_ref, pos_ref, invf_ref, inner_ref, qd_ref, kd_ref, cd_ref,
              o_ref, state):
    @pl.when(pl.program_id(1) == 0)
    def _():
        state[...] = jnp.zeros_like(state)

    half = RET_HEAD // 2
    ang = pos_ref[...].astype(F32) * invf_ref[...]
    cos = jnp.cos(ang)
    sin = jnp.sin(ang)

    def rope(x):
        x1, x2 = x[:, :half], x[:, half:]
        return jnp.concatenate([x1 * cos - x2 * sin, x1 * sin + x2 * cos], axis=-1)

    for hh in range(heads):
        sl = slice(hh * RET_HEAD, (hh + 1) * RET_HEAD)
        q = rope(q_ref[:, sl])
        k = rope(k_ref[:, sl]) * (RET_HEAD ** -0.5)
        v = v_ref[:, sl].astype(BF16)
        g = g_ref[:, sl]
        qb = q.astype(BF16)
        scores = _dot_nt(qb, k.astype(BF16)) * inner_ref[hh]
        st = state[hh]
        o = _dot(scores.astype(BF16), v) + _dot(qb, st.astype(BF16)) * qd_ref[hh]
        state[hh] = st * cd_ref[hh] + _dot_tn((k * kd_ref[hh]).astype(BF16), v)
        on = o * lax.rsqrt(jnp.mean(o * o, axis=-1, keepdims=True) + RET_GN_EPS)
        o_ref[:, sl] = (on * (g * jax.nn.sigmoid(g))).astype(BF16)


def _retention(p, positions, bsz, seq, col0, width):
    t = p.shape[0]
    heads = width // RET_HEAD
    c = RET_CHUNK
    nc = seq // c
    half = RET_HEAD // 2
    cb = col0 // width
    inv_freq = (ROPE_BASE ** (-jnp.arange(half, dtype=F32) / half)).reshape(1, half)
    log_gamma = jnp.log(1.0 - 2.0 ** (-5.0 - jnp.arange(heads, dtype=F32)))
    idx = jnp.arange(c, dtype=F32)
    diff = idx[:, None] - idx[None, :]
    inner = jnp.where(diff >= 0, jnp.exp(log_gamma[:, None, None] * jnp.maximum(diff, 0.0)), 0.0)
    q_decay = jnp.exp(log_gamma[:, None] * (idx + 1.0))[:, :, None]
    k_decay = jnp.exp(log_gamma[:, None] * (c - 1.0 - idx))[:, :, None]
    chunk_decay = jnp.broadcast_to(jnp.exp(log_gamma * c)[:, None, None], (heads, 1, RET_HEAD))

    def pcol(m):
        return pl.BlockSpec((c, width), lambda b, i: (b * nc + i, cb + m))

    def full(a):
        return pl.BlockSpec(a.shape, lambda b, i: (0,) * a.ndim)

    consts = (inv_freq, inner, q_decay, k_decay, chunk_decay)
    return pl.pallas_call(
        functools.partial(_ret_body, heads),
        grid=(bsz, nc),
        in_specs=[pcol(0), pcol(1), pcol(2), pcol(3),
                  pl.BlockSpec((c, 1), lambda b, i: (b * nc + i, 0))] + [full(a) for a in consts],
        out_specs=pl.BlockSpec((c, width), lambda b, i: (b * nc + i, 0)),
        out_shape=jax.ShapeDtypeStruct((t, width), BF16),
        scratch_shapes=[pltpu.VMEM((heads, RET_HEAD, RET_HEAD), F32)],
        compiler_params=_cp("parallel", "arbitrary"),
        name="retention",
    )(p, p, p, p, positions.reshape(t, 1), *consts)


def _outproj_body(ya_ref, yb_ref, wa_ref, wb_ref, x_ref, g_ref, o_ref):
    mix = _dot(ya_ref[...], wa_ref[...]) + _dot(yb_ref[...], wb_ref[...])
    o_ref[...] = x_ref[...] + g_ref[...] * mix


def _outproj(ya, yb, wa, wb, x2d, gate, seq):
    t, d = x2d.shape
    ka, kb = ya.shape[1], yb.shape[1]
    tm, tn = min(1024, seq), min(512, d)
    tps = seq // tm
    return pl.pallas_call(
        _outproj_body,
        grid=(t // tm, d // tn),
        in_specs=[pl.BlockSpec((tm, ka), lambda i, j: (i, 0)),
                  pl.BlockSpec((tm, kb), lambda i, j: (i, 0)),
                  pl.BlockSpec((ka, tn), lambda i, j: (0, j)),
                  pl.BlockSpec((kb, tn), lambda i, j: (0, j)),
                  pl.BlockSpec((tm, tn), lambda i, j: (i, j)),
                  pl.BlockSpec((None, 1, tn), lambda i, j: (i // tps, 0, j))],
        out_specs=pl.BlockSpec((tm, tn), lambda i, j: (i, j)),
        out_shape=jax.ShapeDtypeStruct((t, d), F32),
        compiler_params=_cp("parallel", "arbitrary"),
        name="out_proj",
    )(ya, yb, wa, wb, x2d, gate)


def _pack_bf16_pairs(h):
    half = h.shape[1] // 2
    hi = lax.bitcast_convert_type(h[:, :half].astype(BF16).astype(F32), jnp.int32)
    lo = lax.bitcast_convert_type(h[:, half:].astype(BF16).astype(F32), jnp.int32)
    return hi | lax.shift_right_logical(lo, 16)


def _unpack_bf16_pairs(w):
    hi = lax.bitcast_convert_type(w & -65536, F32).astype(BF16)
    lo = lax.bitcast_convert_type(w << 16, F32).astype(BF16)
    return hi, lo


def _router_body(x_ref, nw_ref, sc_ref, sh_ref, wr_ref, br_ref, tri_ref,
                 h_ref, idx_ref, gate_ref, rank_ref, cnt_ref, carry):
    @pl.when(pl.program_id(0) == 0)
    def _():
        carry[...] = jnp.zeros_like(carry)

    h = _rms(x_ref[...], nw_ref[...]) * (1.0 + sc_ref[...]) + sh_ref[...]
    h_ref[...] = _pack_bf16_pairs(h)
    logits = jnp.dot(h, wr_ref[...], precision=lax.Precision.HIGHEST, preferred_element_type=F32) + br_ref[...]
    lane = lax.broadcasted_iota(jnp.int32, logits.shape, 1)
    vals, idxs = [], []
    rest = logits
    for _ in range(TOP_K):
        m = jnp.max(rest, axis=-1, keepdims=True)
        ix = jnp.min(jnp.where(rest == m, lane, LANES), axis=-1, keepdims=True)
        vals.append(m)
        idxs.append(ix)
        rest = jnp.where(lane == ix, -jnp.inf, rest)
    exps = [jnp.exp(vv - vals[0]) for vv in vals]
    den = exps[0]
    for e in exps[1:]:
        den = den + e
    hot = (lane == idxs[0]).astype(F32)
    for ix in idxs[1:]:
        hot = hot + (lane == ix).astype(F32)
    before = _dot(tri_ref[...], hot.astype(BF16)) + carry[0:1, :]
    carry[0:1, :] = carry[0:1, :] + jnp.sum(hot, axis=0, keepdims=True)
    idx_out = jnp.zeros(logits.shape, jnp.int32)
    gate_out = jnp.zeros(logits.shape, F32)
    rank_out = jnp.zeros(logits.shape, F32)
    for kk in range(TOP_K):
        rk = jnp.sum(jnp.where(lane == idxs[kk], before, 0.0), axis=-1, keepdims=True)
        idx_out = jnp.where(lane == kk, idxs[kk], idx_out)
        gate_out = jnp.where(lane == kk, exps[kk] / den, gate_out)
        rank_out = jnp.where(lane == kk, rk, rank_out)
    idx_ref[...] = idx_out
    gate_ref[...] = gate_out
    rank_ref[...] = rank_out.astype(jnp.int32)
    cnt_ref[...] = jnp.broadcast_to(carry[0:1, :], cnt_ref.shape).astype(jnp.int32)


def _router(x2d, nw, scale, shift, w_router, b_router, seq):
    t, d = x2d.shape
    e = w_router.shape[1]
    tm = min(256, seq)
    tps = seq // tm
    wr = jnp.zeros((d, LANES), F32).at[:, :e].set(w_router)
    br = jnp.full((1, LANES), -1e30, F32).at[0, :e].set(b_router)
    tri = (jnp.arange(tm)[None, :] < jnp.arange(tm)[:, None]).astype(BF16)
    row_d = pl.BlockSpec((1, d), lambda i: (0, 0))
    mod = pl.BlockSpec((None, 1, d), lambda i: (i // tps, 0, 0))
    lane_out = pl.BlockSpec((tm, LANES), lambda i: (i, 0))
    return pl.pallas_call(
        _router_body,
        grid=(t // tm,),
        in_specs=[pl.BlockSpec((tm, d), lambda i: (i, 0)), row_d, mod, mod,
                  pl.BlockSpec((d, LANES), lambda i: (0, 0)),
                  pl.BlockSpec((1, LANES), lambda i: (0, 0)),
                  pl.BlockSpec((tm, tm), lambda i: (0, 0))],
        out_specs=[pl.BlockSpec((tm, d // 2), lambda i: (i, 0)), lane_out, lane_out, lane_out,
                   pl.BlockSpec((8, LANES), lambda i: (0, 0))],
        out_shape=[jax.ShapeDtypeStruct((t, d // 2), jnp.int32),
                   jax.ShapeDtypeStruct((t, LANES), jnp.int32),
                   jax.ShapeDtypeStruct((t, LANES), F32),
                   jax.ShapeDtypeStruct((t, LANES), jnp.int32),
                   jax.ShapeDtypeStruct((8, LANES), jnp.int32)],
        scratch_shapes=[pltpu.VMEM((8, LANES), F32)],
        compiler_params=_cp("arbitrary"),
        name="norm2_router",
    )(x2d, nw, scale, shift, wr, br, tri)


def _row_copy(src, dst, sem):
    return pltpu.make_async_copy(src, dst, sem)


def _dispatch_body(tm, dest_ref, h_ref, rows_in, rows_out, sem):
    del rows_in

    def issue(r, carry_):
        for kk in range(TOP_K):
            d = dest_ref[0, r * TOP_K + kk]
            _row_copy(h_ref.at[pl.ds(r, 1)], rows_out.at[pl.ds(d, 1)], sem).start()
        return carry_

    def drain(r, carry_):
        for _ in range(TOP_K):
            _row_copy(h_ref.at[pl.ds(0, 1)], rows_out.at[pl.ds(0, 1)], sem).wait()
        return carry_

    lax.fori_loop(0, tm, issue, 0)
    lax.fori_loop(0, tm, drain, 0)


def _dispatch(h2, dest, cap):
    t, d = h2.shape
    tm = min(128, t)
    dest3 = dest.reshape(t // tm, 1, tm * TOP_K)
    return pl.pallas_call(
        functools.partial(_dispatch_body, tm),
        grid=(t // tm,),
        in_specs=[pl.BlockSpec((None, 1, tm * TOP_K), lambda i: (i, 0, 0), memory_space=pltpu.SMEM),
                  pl.BlockSpec((tm, d), lambda i: (i, 0)),
                  pl.BlockSpec(memory_space=pl.ANY)],
        out_specs=pl.BlockSpec(memory_space=pl.ANY),
        out_shape=jax.ShapeDtypeStruct((cap, d), h2.dtype),
        scratch_shapes=[pltpu.SemaphoreType.DMA],
        input_output_aliases={2: 0},
        compiler_params=_cp("arbitrary"),
        name="moe_dispatch",
    )(dest3, h2, jnp.zeros((cap, d), h2.dtype))


def _moe_steps(blk_start, nblk, n_active, n_tiles, nb):
    n_experts = blk_start.shape[0]
    s = jnp.arange(n_tiles * nb, dtype=jnp.int32)
    step_end = n_tiles * (blk_start + nblk)
    e = jnp.minimum(jnp.sum(step_end[None, :] <= s[:, None], axis=1), n_experts - 1).astype(jnp.int32)
    local = s - n_tiles * blk_start[e]
    per = jnp.maximum(nblk[e], 1)
    tile = local // per
    rb = blk_start[e] + local % per
    n_steps = (n_tiles * n_active).astype(jnp.int32)
    active = s < n_steps
    last = n_steps - 1
    spare = s - n_steps
    cols = (jnp.where(active, e, e[last]), jnp.where(active, tile, tile[last]), jnp.where(active, rb, rb[last]),
            jnp.where(active, tile, spare % n_tiles), jnp.where(active, rb, n_active + spare // n_tiles),
            (local % per == 0))
    return tuple(x.astype(jnp.int32) for x in cols) + (n_steps.reshape(1),)


def _gate_up_body(se, sti, sri, sto, sro, first, n_steps, x_ref, wg_ref, wu_ref, bg_ref, bu_ref,
                  act_ref, wgb, wub):
    s = pl.program_id(0)

    @pl.when(s >= n_steps[0])
    def _():
        act_ref[...] = jnp.zeros_like(act_ref)

    @pl.when(s < n_steps[0])
    def _():
        @pl.when(first[s] == 1)
        def _():
            wgb[...] = wg_ref[...].astype(BF16)
            wub[...] = wu_ref[...].astype(BF16)

        xa, xb = _unpack_bf16_pairs(x_ref[...])
        half = xa.shape[1]

        def proj(w, b_ref):
            return _dot(xa, w[:half, :]) + _dot(xb, w[half:, :]) + b_ref[...]

        gate = jnp.minimum(proj(wgb, bg_ref), SWIGLU_LIMIT)
        up = jnp.clip(proj(wub, bu_ref), -SWIGLU_LIMIT, SWIGLU_LIMIT)
        act_ref[...] = ((up + 1.0) * gate * jax.nn.sigmoid(SWIGLU_ALPHA * gate)).astype(BF16)


def _gate_up(rows, steps, wgu, bgu):
    cap, half = rows.shape
    d = 2 * half
    e, _, two_de = wgu.shape
    de = two_de // 2
    bm = MOE_ROWS
    tn = min(MOE_UP_TILE, de)
    nj = de // tn
    grid_spec = pltpu.PrefetchScalarGridSpec(
        num_scalar_prefetch=7,
        grid=(nj * (cap // bm),),
        in_specs=[pl.BlockSpec((bm, half), lambda s, se, sti, sri, *_: (sri[s], 0)),
                  pl.BlockSpec((None, d, tn), lambda s, se, sti, *_: (se[s], 0, sti[s])),
                  pl.BlockSpec((None, d, tn), lambda s, se, sti, *_: (se[s], 0, nj + sti[s])),
                  pl.BlockSpec((None, 1, tn), lambda s, se, sti, *_: (se[s], 0, sti[s])),
                  pl.BlockSpec((None, 1, tn), lambda s, se, sti, *_: (se[s], 0, nj + sti[s]))],
        out_specs=pl.BlockSpec((bm, tn), lambda s, se, sti, sri, sto, sro, *_: (sro[s], sto[s])),
        scratch_shapes=[pltpu.VMEM((d, tn), BF16), pltpu.VMEM((d, tn), BF16)],
    )
    bgu3 = bgu.reshape(e, 1, two_de)
    return pl.pallas_call(
        _gate_up_body,
        grid_spec=grid_spec,
        out_shape=jax.ShapeDtypeStruct((cap, de), BF16),
        compiler_params=_cp("arbitrary"),
        name="moe_gate_up",
    )(*steps, rows, wgu, wgu, bgu3, bgu3)


def _down_body(se, sti, sri, sto, sro, first, n_steps, a_ref, wd_ref, bd_ref, o_ref, wdb):
    s = pl.program_id(0)

    @pl.when(s >= n_steps[0])
    def _():
        o_ref[...] = jnp.zeros_like(o_ref)

    @pl.when(s < n_steps[0])
    def _():
        @pl.when(first[s] == 1)
        def _():
            wdb[...] = wd_ref[...].astype(BF16)

        o_ref[...] = _dot(a_ref[...], wdb[...]) + bd_ref[...]


def _down(act, steps, wd, bd, tn):
    cap, de = act.shape
    e, _, d = wd.shape
    bm = MOE_ROWS
    grid_spec = pltpu.PrefetchScalarGridSpec(
        num_scalar_prefetch=7,
        grid=((d // tn) * (cap // bm),),
        in_specs=[pl.BlockSpec((bm, de), lambda s, se, sti, sri, *_: (sri[s], 0)),
                  pl.BlockSpec((None, de, tn), lambda s, se, sti, *_: (se[s], 0, sti[s])),
                  pl.BlockSpec((None, 1, tn), lambda s, se, sti, *_: (se[s], 0, sti[s]))],
        out_specs=pl.BlockSpec((bm, tn), lambda s, se, sti, sri, sto, sro, *_: (sro[s], sto[s])),
        scratch_shapes=[pltpu.VMEM((de, tn), BF16)],
    )
    return pl.pallas_call(
        _down_body,
        grid_spec=grid_spec,
        out_shape=jax.ShapeDtypeStruct((cap, d), F32),
        compiler_params=_cp("arbitrary"),
        name="moe_down",
    )(*steps, act, wd, bd.reshape(e, 1, d))


def _combine_body(tm, n_tiles, dest_ref, dest_next_ref, rows_ref, gates_ref, x_ref, g2_ref, nw_ref, o_ref,
                  buf, sem):
    i = pl.program_id(0)
    slot = i % 2

    def fetch(dref, sl):
        def issue(r, carry_):
            for kk in range(TOP_K):
                d = dref[0, r * TOP_K + kk]
                _row_copy(rows_ref.at[pl.ds(d, 1)], buf.at[sl, kk, pl.ds(r, 1)], sem.at[sl]).start()
            return carry_

        lax.fori_loop(0, tm, issue, 0)

    @pl.when(i == 0)
    def _():
        fetch(dest_ref, 0)

    @pl.when(i + 1 < n_tiles)
    def _():
        fetch(dest_next_ref, 1 - slot)

    def drain(r, carry_):
        for kk in range(TOP_K):
            _row_copy(rows_ref.at[pl.ds(0, 1)], buf.at[slot, kk, pl.ds(0, 1)], sem.at[slot]).wait()
        return carry_

    lax.fori_loop(0, tm, drain, 0)
    gates = gates_ref[...]
    y = gates[:, 0:1] * buf[slot, 0]
    for kk in range(1, TOP_K):
        y = y + gates[:, kk:kk + 1] * buf[slot, kk]
    x2 = x_ref[...] + g2_ref[...] * y
    o_ref[...] = _rms(x2, nw_ref[...])


def _combine(out_rows, dest, gates, x1, gate2, normf_w, seq):
    t, d = x1.shape
    tm = min(64, seq)
    tps = seq // tm
    n_tiles = t // tm
    dest3 = dest.reshape(n_tiles, 1, tm * TOP_K)
    return pl.pallas_call(
        functools.partial(_combine_body, tm, n_tiles),
        grid=(n_tiles,),
        in_specs=[pl.BlockSpec((None, 1, tm * TOP_K), lambda i: (i, 0, 0), memory_space=pltpu.SMEM),
                  pl.BlockSpec((None, 1, tm * TOP_K), lambda i: (jnp.minimum(i + 1, n_tiles - 1), 0, 0),
                               memory_space=pltpu.SMEM),
                  pl.BlockSpec(memory_space=pl.ANY),
                  pl.BlockSpec((tm, LANES), lambda i: (i, 0)),
                  pl.BlockSpec((tm, d), lambda i: (i, 0)),
                  pl.BlockSpec((None, 1, d), lambda i: (i // tps, 0, 0)),
                  pl.BlockSpec((1, d), lambda i: (0, 0))],
        out_specs=pl.BlockSpec((tm, d), lambda i: (i, 0)),
        out_shape=jax.ShapeDtypeStruct((t, d), F32),
        scratch_shapes=[pltpu.VMEM((2, TOP_K, tm, d), F32), pltpu.SemaphoreType.DMA((2,))],
        compiler_params=_cp("arbitrary"),
        name="moe_combine",
    )(dest3, dest3, out_rows, gates, x1, gate2, normf_w.reshape(1, d))


def _moe_plan(idx, rank, counts, n_experts):
    bm = MOE_ROWS
    n_assign = idx.shape[0] * TOP_K
    nblk = (counts + bm - 1) // bm
    blk_end = jnp.cumsum(nblk)
    blk_start = blk_end - nblk
    dest = (bm * blk_start[idx] + rank).astype(jnp.int32)
    nb = (n_assign + n_experts * (bm - 1) + bm - 1) // bm
    return dest, blk_start.astype(jnp.int32), nblk.astype(jnp.int32), blk_end[-1].astype(jnp.int32), nb


def kernel(x, c, positions, w_ada, b_ada, norm1_w, w_in, rwkv_mu_rkvg, rwkv_mu_wa, rwkv_w0, rwkv_w1, rwkv_w2, rwkv_a0, rwkv_a1, rwkv_a2, rwkv_k_k, rwkv_k_a, rwkv_r_k, rwkv_lnx_w, rwkv_lnx_b, w_out, norm2_w, w_router, b_router, w_gate_up, b_gate_up, w_down, b_down, normf_w):
    bsz, seq, d = x.shape
    t = bsz * seq
    depth = w_ada.shape[0]
    rw = rwkv_w0.shape[1]
    ret_w = d - rw
    n_experts = w_router.shape[2]
    x2d = x.reshape(t, d)
    for l in range(depth):
        mod = _adaln(c, w_ada[l], b_ada[l])
        shift1, scale1, gate1, shift2, scale2, gate2 = [m.reshape(bsz, 1, d) for m in jnp.split(mod, 6, axis=-1)]
        h, tw, la = _norm1_lora(x2d, norm1_w[l].reshape(1, d), scale1, shift1,
                                rwkv_mu_wa[l, 0].reshape(1, d), rwkv_mu_wa[l, 1].reshape(1, d),
                                rwkv_w1[l].astype(BF16), rwkv_a1[l].astype(BF16), seq)
        p = _matmul(h, w_in[l].astype(BF16))
        mu = rwkv_mu_rkvg[l].reshape(4, 1, rw)
        prm = (mu[0], mu[1], mu[2], mu[3], rwkv_w0[l].reshape(1, rw), rwkv_a0[l].reshape(1, rw),
               rwkv_k_k[l].reshape(1, rw), rwkv_k_a[l].reshape(1, rw), rwkv_r_k[l].reshape(1, rw),
               rwkv_lnx_w[l].reshape(1, rw), rwkv_lnx_b[l].reshape(1, rw))
        y_rwkv = _rwkv(p, tw, la, prm, rwkv_w2[l].astype(BF16), rwkv_a2[l].astype(BF16), bsz, seq, rw)
        y_ret = _retention(p, positions, bsz, seq, 4 * rw, ret_w)
        wo = w_out[l].astype(BF16)
        x2d = _outproj(y_rwkv, y_ret, wo[:rw], wo[rw:], x2d, gate1, seq)
        h2, idx, gates, rank, counts = _router(x2d, norm2_w[l].reshape(1, d), scale2, shift2,
                                               w_router[l], b_router[l], seq)
        dest, blk_start, nblk, n_active, nb = _moe_plan(idx[:, :TOP_K], rank[:, :TOP_K], counts[0, :n_experts], n_experts)
        rows = _dispatch(h2, dest, nb * MOE_ROWS)
        de = w_down.shape[2]
        up_tiles = de // min(MOE_UP_TILE, de)
        act = _gate_up(rows, _moe_steps(blk_start, nblk, n_active, up_tiles, nb), w_gate_up[l], b_gate_up[l])
        tn_down = min(MOE_DOWN_TILE, d)
        out_rows = _down(act, _moe_steps(blk_start, nblk, n_active, d // tn_down, nb), w_down[l], b_down[l], tn_down)
        if l + 1 < depth:
            raise NotImplementedError("the final norm is fused into the last layer's combine")
        out = _combine(out_rows, dest, gates, x2d, gate2, normf_w, seq)
    return out.reshape(bsz, seq, d)
```

```python
import functools

import jax
import jax.numpy as jnp
from jax import lax
from jax.experimental import pallas as pl
from jax.experimental.pallas import tpu as pltpu

F32 = jnp.float32
BF16 = jnp.bfloat16

RWKV_HEAD = 64
RET_HEAD = 256
RET_CHUNK = 128
ROPE_BASE = 10000.0
TOP_K = 4
SWIGLU_LIMIT = 7.0
SWIGLU_ALPHA = 1.702
NORM_EPS = 1e-5
RWKV_GN_EPS = 64e-5
RET_GN_EPS = 1e-6

LANES = 128
RWKV_CHUNK = 64
RWKV_GROUP = 256
HEADS_PER_GROUP = RWKV_GROUP // RWKV_HEAD
MOE_ROWS = 512
MOE_UP_TILE = 256
MOE_DOWN_TILE = 2048
VMEM_LIMIT = 56 << 20


def _cp(*sem):
    return pltpu.CompilerParams(dimension_semantics=sem, vmem_limit_bytes=VMEM_LIMIT)


def _dot(a, b):
    return jnp.dot(a, b, preferred_element_type=F32)


def _dot_nt(a, b):
    return lax.dot_general(a, b, (((1,), (1,)), ((), ())), preferred_element_type=F32)


def _dot_tn(a, b):
    return lax.dot_general(a, b, (((0,), (0,)), ((), ())), preferred_element_type=F32)


def _split2(x):
    hi = x.astype(BF16)
    lo = (x - hi.astype(F32)).astype(BF16)
    return hi, lo


def _rms(x, w):
    return x * lax.rsqrt(jnp.mean(x * x, axis=-1, keepdims=True) + NORM_EPS) * w


def _adaln_body(c_ref, w_ref, b_ref, o_ref):
    c = c_ref[...]
    ca = c * jax.nn.sigmoid(c)
    o_ref[...] = _dot(ca.astype(BF16), w_ref[...].astype(BF16)) + b_ref[...]


def _adaln(c, w_ada, b_ada):
    bsz, d = c.shape
    n = w_ada.shape[1]
    tn = min(512, n)
    c8 = jnp.zeros((8, d), F32).at[:bsz].set(c)
    out = pl.pallas_call(
        _adaln_body,
        grid=(n // tn,),
        in_specs=[pl.BlockSpec((8, d), lambda j: (0, 0)),
                  pl.BlockSpec((d, tn), lambda j: (0, j)),
                  pl.BlockSpec((1, tn), lambda j: (0, j))],
        out_specs=pl.BlockSpec((8, tn), lambda j: (0, j)),
        out_shape=jax.ShapeDtypeStruct((8, n), F32),
        compiler_params=_cp("arbitrary"),
        name="adaln",
    )(c8, w_ada, b_ada.reshape(1, n))
    return out[:bsz]


def _norm1_body(tiles_per_seq, x_ref, nw_ref, sc_ref, sh_ref, muw_ref, mua_ref, w1_ref, a1_ref,
                h_ref, tw_ref, la_ref, carry):
    i = pl.program_id(0)

    @pl.when(i % tiles_per_seq == 0)
    def _():
        carry[...] = jnp.zeros_like(carry)

    x = x_ref[...]
    tm = x.shape[0]
    h = _rms(x, nw_ref[...]) * (1.0 + sc_ref[...]) + sh_ref[...]
    row = lax.broadcasted_iota(jnp.int32, h.shape, 0)
    h_prev = jnp.where(row == 0, carry[0:1, :], pltpu.roll(h, 1, 0))
    carry[0:1, :] = h[tm - 1:tm, :]
    dh = h_prev - h
    xw = h + dh * muw_ref[...]
    xa = h + dh * mua_ref[...]
    tw_ref[...] = jnp.tanh(_dot(xw.astype(BF16), w1_ref[...]))
    la_ref[...] = _dot(xa.astype(BF16), a1_ref[...])
    h_ref[...] = h.astype(BF16)


def _norm1_lora(x2d, nw, scale, shift, mu_w, mu_a, w1b, a1b, seq):
    t, d = x2d.shape
    tm = min(256, seq)
    tps = seq // tm
    r = w1b.shape[1]
    row_d = pl.BlockSpec((1, d), lambda i: (0, 0))
    mod = pl.BlockSpec((None, 1, d), lambda i: (i // tps, 0, 0))
    lora = pl.BlockSpec((d, r), lambda i: (0, 0))
    return pl.pallas_call(
        functools.partial(_norm1_body, tps),
        grid=(t // tm,),
        in_specs=[pl.BlockSpec((tm, d), lambda i: (i, 0)), row_d, mod, mod, row_d, row_d, lora, lora],
        out_specs=[pl.BlockSpec((tm, d), lambda i: (i, 0)),
                   pl.BlockSpec((tm, r), lambda i: (i, 0)),
                   pl.BlockSpec((tm, r), lambda i: (i, 0))],
        out_shape=[jax.ShapeDtypeStruct((t, d), BF16),
                   jax.ShapeDtypeStruct((t, r), F32),
                   jax.ShapeDtypeStruct((t, r), F32)],
        scratch_shapes=[pltpu.VMEM((8, d), F32)],
        compiler_params=_cp("arbitrary"),
        name="norm1_lora",
    )(x2d, nw, scale, shift, mu_w, mu_a, w1b, a1b)


def _matmul_body(a_ref, b_ref, o_ref):
    o_ref[...] = _dot(a_ref[...], b_ref[...])


def _matmul(a, b):
    m, k = a.shape
    n = b.shape[1]
    tm, tn = min(1024, m), min(512, n)
    return pl.pallas_call(
        _matmul_body,
        grid=(m // tm, n // tn),
        in_specs=[pl.BlockSpec((tm, k), lambda i, j: (i, 0)),
                  pl.BlockSpec((k, tn), lambda i, j: (0, j))],
        out_specs=pl.BlockSpec((tm, tn), lambda i, j: (i, j)),
        out_shape=jax.ShapeDtypeStruct((m, n), F32),
        compiler_params=_cp("parallel", "arbitrary"),
        name="in_proj",
    )(a, b)


def _rwkv_body(tb, pr_ref, pk_ref, pv_ref, pg_ref, tw_ref, la_ref,
               mur_ref, muk_ref, muv_ref, mug_ref, w0_ref, a0_ref, kk_ref, ka_ref, rk_ref,
               lnw_ref, lnb_ref, w2_ref, a2_ref,
               bd_ref, mstk_ref, sm_ref, im_ref, tri_ref, eye_ref, lvl_ref,
               y_ref,
               carry, state):
    i = pl.program_id(2)
    c = RWKV_CHUNK

    @pl.when(i == 0)
    def _():
        carry[...] = jnp.zeros_like(carry)
        state[...] = jnp.zeros_like(state)

    row = lax.broadcasted_iota(jnp.int32, (tb, RWKV_GROUP), 0)

    def shift_mix(ref, slot, mu):
        cur = ref[...]
        prev = jnp.where(row == 0, carry[slot:slot + 1, :], pltpu.roll(cur, 1, 0))
        carry[slot:slot + 1, :] = cur[tb - 1:tb, :]
        return cur + (prev - cur) * mu

    bd = bd_ref[...]

    def seg_sum(x):
        hi, lo = _split2(x)
        return _dot(hi, bd) + _dot(lo, bd)

    r = shift_mix(pr_ref, 0, mur_ref[...])
    k = shift_mix(pk_ref, 1, muk_ref[...])
    v = shift_mix(pv_ref, 2, muv_ref[...])
    g = shift_mix(pg_ref, 3, mug_ref[...])

    z = -(w0_ref[...] + _dot(tw_ref[...].astype(BF16), w2_ref[...]))
    softplus = jnp.maximum(z, 0.0) + jnp.log(1.0 + jnp.exp(-jnp.abs(z)))
    lw = -jnp.exp(-softplus - 0.5)
    a_icl = jax.nn.sigmoid(a0_ref[...] + _dot(la_ref[...].astype(BF16), a2_ref[...]))
    kk = k * kk_ref[...]
    kk = kk / jnp.maximum(jnp.sqrt(seg_sum(kk * kk)), 1e-12)
    k = k * (1.0 + (a_icl - 1.0) * ka_ref[...])
    a_in = -kk
    b_in = kk * a_icl

    mstk = mstk_ref[...]
    strict = sm_ref[...] > 0.0
    incl = im_ref[...] > 0.0
    same_head = bd > 0
    tri = tri_ref[...]

    def stack(x):
        return jnp.concatenate([x.astype(BF16)] * HEADS_PER_GROUP, axis=0) * mstk

    n_ch = tb // c

    def each(fn, *cols):
        return [fn(*args) for args in zip(*cols)]

    rows = [slice(ci * c, (ci + 1) * c) for ci in range(n_ch)]
    lw_c = [lw[sl] for sl in rows]

    def cumsum(x):
        hi, lo = _split2(x)
        return _dot(tri, hi) + _dot(tri, lo)

    cum = each(cumsum, lw_c)
    cum_end = [x[c - 1:c, :] for x in cum]
    p_inv = each(lambda x: jnp.exp(-x), cum)
    p_end = each(lambda x, e: jnp.exp(e - x), cum, cum_end)
    a_d = each(lambda sl, x, l: (a_in[sl] * jnp.exp(x - l)).astype(BF16), rows, cum, lw_c)
    r_d = each(lambda sl, x: r[sl] * jnp.exp(x), rows, cum)
    r_b = each(lambda x: x.astype(BF16), r_d)
    a_t = each(stack, a_d)
    b_t = each(lambda sl, p: stack(b_in[sl] * p), rows, p_inv)
    k_t = each(lambda sl, p: stack(k[sl] * p), rows, p_inv)
    v_t = each(lambda sl: stack(v[sl]), rows)
    b_h = each(lambda sl, p: (b_in[sl] * p).astype(BF16), rows, p_end)
    uk_r = each(lambda sl, p: (k[sl] * p).astype(BF16), rows, p_end)
    a_ab = each(lambda x, y: jnp.where(strict, _dot_nt(x, y), 0.0), a_d, b_t)
    a_ak = each(lambda x, y: jnp.where(strict, _dot_nt(x, y), 0.0).astype(BF16), a_d, k_t)
    a_rb = each(lambda x, y: jnp.where(incl, _dot_nt(x, y), 0.0).astype(BF16), r_b, b_t)
    a_rk = each(lambda x, y: jnp.where(incl, _dot_nt(x, y), 0.0).astype(BF16), r_b, k_t)
    inv = each(lambda x: eye_ref[...] + x * lvl_ref[0], a_ab)
    for lv in range(1, c.bit_length() - 1):
        inv_bd = each(stack, inv)
        w = each(lambda x, t: _dot((x * lvl_ref[lv]).astype(BF16), t), a_ab, inv_bd)
        inv = each(lambda x, ww: x + _dot(x.astype(BF16), stack(ww)), inv, w)
    invb = each(lambda x: x.astype(BF16), inv)
    a_p = each(_dot, invb, a_t)
    akv = each(_dot, a_ak, v_t)
    u_loc = each(lambda t, x: _dot(t, stack(x)), invb, akv)
    r_p = each(lambda x, m, y: (x + _dot(m, stack(y))).astype(BF16), r_d, a_rb, a_p)
    y_loc = each(lambda m, x, n, y: _dot(m, stack(x)) + _dot(n, y), a_rb, u_loc, a_rk, v_t)
    phi = each(lambda x, y: jnp.where(same_head, _dot_tn(x.astype(BF16), y), 0.0).astype(BF16), a_p, b_h)
    s_loc = each(lambda x, y, sl, z: jnp.where(
        same_head,
        _dot_tn(jnp.concatenate([x.astype(BF16), v[sl].astype(BF16)], axis=0), jnp.concatenate([y, z], axis=0)),
        0.0), u_loc, b_h, rows, uk_r)

    s = state[...]
    ys = []
    for ci in range(n_ch):
        sb = s.astype(BF16)
        ys.append(_dot_nt(r_p[ci], sb) + y_loc[ci])
        s = s * jnp.exp(cum_end[ci]) + _dot(sb, phi[ci]) + s_loc[ci]
    state[...] = s

    y = jnp.concatenate(ys, axis=0)
    inv_n = 1.0 / RWKV_HEAD
    mean = seg_sum(y) * inv_n
    dlt = y - mean
    var = seg_sum(dlt * dlt) * inv_n
    yn = dlt * lax.rsqrt(var + RWKV_GN_EPS) * lnw_ref[...] + lnb_ref[...]
    bonus = seg_sum(r * k * rk_ref[...]) * v
    y_ref[...] = ((yn + bonus) * jax.nn.sigmoid(g)).astype(BF16)


def _rwkv_consts():
    n = RWKV_GROUP
    c = RWKV_CHUNK
    lane = jnp.arange(n)
    head = lane // RWKV_HEAD
    blk = lane // c
    bd = (head[:, None] == head[None, :]).astype(BF16)
    mstk = (blk[:, None] == head[None, :]).astype(BF16)
    t_idx = jnp.arange(c)[:, None]
    s_idx = (lane % c)[None, :]
    strict = (s_idx < t_idx).astype(F32)
    incl = (s_idx <= t_idx).astype(F32)
    eye = (s_idx == t_idx).astype(F32)
    tri = (jnp.arange(c)[None, :] <= jnp.arange(c)[:, None]).astype(BF16)
    lvls = []
    for lv in range(c.bit_length() - 1):
        m = 1 << lv
        hi_lo = ((t_idx // m) % 2 == 1) & ((s_idx // m) % 2 == 0)
        lvls.append((hi_lo & (t_idx // (2 * m) == s_idx // (2 * m))).astype(F32))
    return bd, mstk, strict, incl, tri, eye, jnp.stack(lvls)


def _rwkv(p, tw, la, prm, w2b, a2b, bsz, seq, rw):
    t = p.shape[0]
    g = rw // RWKV_GROUP
    tb = min(512, seq)
    nt = seq // tb
    r = tw.shape[1]
    bd, mstk, strict, incl, tri, eye, lvls = _rwkv_consts()

    def pcol(m):
        return pl.BlockSpec((tb, RWKV_GROUP), lambda b, gg, i: (b * nt + i, m * g + gg))

    lor = pl.BlockSpec((tb, r), lambda b, gg, i: (b * nt + i, 0))
    chan = pl.BlockSpec((1, RWKV_GROUP), lambda b, gg, i: (0, gg))
    lw2 = pl.BlockSpec((r, RWKV_GROUP), lambda b, gg, i: (0, gg))
    sq = pl.BlockSpec((RWKV_GROUP, RWKV_GROUP), lambda b, gg, i: (0, 0))
    tri_spec = pl.BlockSpec((RWKV_CHUNK, RWKV_CHUNK), lambda b, gg, i: (0, 0))
    wide = pl.BlockSpec((RWKV_CHUNK, RWKV_GROUP), lambda b, gg, i: (0, 0))
    return pl.pallas_call(
        functools.partial(_rwkv_body, tb),
        grid=(bsz, g, nt),
        in_specs=[pcol(0), pcol(1), pcol(2), pcol(3), lor, lor] + [chan] * 11 + [lw2, lw2]
                 + [sq, sq, wide, wide, tri_spec, wide, pl.BlockSpec(lvls.shape, lambda b, gg, i: (0, 0, 0))],
        out_specs=pl.BlockSpec((tb, RWKV_GROUP), lambda b, gg, i: (b * nt + i, gg)),
        out_shape=jax.ShapeDtypeStruct((t, rw), BF16),
        scratch_shapes=[pltpu.VMEM((8, RWKV_GROUP), F32), pltpu.VMEM((RWKV_GROUP, RWKV_GROUP), F32)],
        compiler_params=_cp("parallel", "parallel", "arbitrary"),
        name="rwkv7",
    )(p, p, p, p, tw, la, *prm, w2b, a2b, bd, mstk, strict, incl, tri, eye, lvls)


def _ret_body(heads, q_ref, k_ref, v_ref, g_ref, pos_ref, invf_ref, inner_ref, qd_ref, kd_ref, cd_ref,
              o_ref, state):
    @pl.when(pl.program_id(1) == 0)
    def _():
        state[...] = jnp.zeros_like(state)

    half = RET_HEAD // 2
    ang = pos_ref[...].astype(F32) * invf_ref[...]
    cos = jnp.cos(ang)
    sin = jnp.sin(ang)

    def rope(x):
        x1, x2 = x[:, :half], x[:, half:]
        return jnp.concatenate([x1 * cos - x2 * sin, x1 * sin + x2 * cos], axis=-1)

    for hh in range(heads):
        sl = slice(hh * RET_HEAD, (hh + 1) * RET_HEAD)
        q = rope(q_ref[:, sl])
        k = rope(k_ref[:, sl]) * (RET_HEAD ** -0.5)
        v = v_ref[:, sl].astype(BF16)
        g = g_ref[:, sl]
        qb = q.astype(BF16)
        scores = _dot_nt(qb, k.astype(BF16)) * inner_ref[hh]
        st = state[hh]
        o = _dot(scores.astype(BF16), v) + _dot(qb, st.astype(BF16)) * qd_ref[hh]
        state[hh] = st * cd_ref[hh] + _dot_tn((k * kd_ref[hh]).astype(BF16), v)
        on = o * lax.rsqrt(jnp.mean(o * o, axis=-1, keepdims=True) + RET_GN_EPS)
        o_ref[:, sl] = (on * (g * jax.nn.sigmoid(g))).astype(BF16)


def _retention(p, positions, bsz, seq, col0, width):
    t = p.shape[0]
    heads = width // RET_HEAD
    c = RET_CHUNK
    nc = seq // c
    half = RET_HEAD // 2
    cb = col0 // width
    inv_freq = (ROPE_BASE ** (-jnp.arange(half, dtype=F32) / half)).reshape(1, half)
    log_gamma = jnp.log(1.0 - 2.0 ** (-5.0 - jnp.arange(heads, dtype=F32)))
    idx = jnp.arange(c, dtype=F32)
    diff = idx[:, None] - idx[None, :]
    inner = jnp.where(diff >= 0, jnp.exp(log_gamma[:, None, None] * jnp.maximum(diff, 0.0)), 0.0)
    q_decay = jnp.exp(log_gamma[:, None] * (idx + 1.0))[:, :, None]
    k_decay = jnp.exp(log_gamma[:, None] * (c - 1.0 - idx))[:, :, None]
    chunk_decay = jnp.broadcast_to(jnp.exp(log_gamma * c)[:, None, None], (heads, 1, RET_HEAD))

    def pcol(m):
        return pl.BlockSpec((c, width), lambda b, i: (b * nc + i, cb + m))

    def full(a):
        return pl.BlockSpec(a.shape, lambda b, i: (0,) * a.ndim)

    consts = (inv_freq, inner, q_decay, k_decay, chunk_decay)
    return pl.pallas_call(
        functools.partial(_ret_body, heads),
        grid=(bsz, nc),
        in_specs=[pcol(0), pcol(1), pcol(2), pcol(3),
                  pl.BlockSpec((c, 1), lambda b, i: (b * nc + i, 0))] + [full(a) for a in consts],
        out_specs=pl.BlockSpec((c, width), lambda b, i: (b * nc + i, 0)),
        out_shape=jax.ShapeDtypeStruct((t, width), BF16),
        scratch_shapes=[pltpu.VMEM((heads, RET_HEAD, RET_HEAD), F32)],
        compiler_params=_cp("parallel", "arbitrary"),
        name="retention",
    )(p, p, p, p, positions.reshape(t, 1), *consts)


def _outproj_body(ya_ref, yb_ref, wa_ref, wb_ref, x_ref, g_ref, o_ref):
    mix = _dot(ya_ref[...], wa_ref[...]) + _dot(yb_ref[...], wb_ref[...])
    o_ref[...] = x_ref[...] + g_ref[...] * mix


def _outproj(ya, yb, wa, wb, x2d, gate, seq):
    t, d = x2d.shape
    ka, kb = ya.shape[1], yb.shape[1]
    tm, tn = min(1024, seq), min(512, d)
    tps = seq // tm
    return pl.pallas_call(
        _outproj_body,
        grid=(t // tm, d // tn),
        in_specs=[pl.BlockSpec((tm, ka), lambda i, j: (i, 0)),
                  pl.BlockSpec((tm, kb), lambda i, j: (i, 0)),
                  pl.BlockSpec((ka, tn), lambda i, j: (0, j)),
                  pl.BlockSpec((kb, tn), lambda i, j: (0, j)),
                  pl.BlockSpec((tm, tn), lambda i, j: (i, j)),
                  pl.BlockSpec((None, 1, tn), lambda i, j: (i // tps, 0, j))],
        out_specs=pl.BlockSpec((tm, tn), lambda i, j: (i, j)),
        out_shape=jax.ShapeDtypeStruct((t, d), F32),
        compiler_params=_cp("parallel", "arbitrary"),
        name="out_proj",
    )(ya, yb, wa, wb, x2d, gate)


def _pack_bf16_pairs(h):
    half = h.shape[1] // 2
    hi = lax.bitcast_convert_type(h[:, :half].astype(BF16).astype(F32), jnp.int32)
    lo = lax.bitcast_convert_type(h[:, half:].astype(BF16).astype(F32), jnp.int32)
    return hi | lax.shift_right_logical(lo, 16)


def _unpack_bf16_pairs(w):
    hi = lax.bitcast_convert_type(w & -65536, F32).astype(BF16)
    lo = lax.bitcast_convert_type(w << 16, F32).astype(BF16)
    return hi, lo


def _router_body(x_ref, nw_ref, sc_ref, sh_ref, wrh_ref, wrl_ref, br_ref, tri_ref,
                 h_ref, idx_ref, gate_ref, rank_ref, cnt_ref, carry):
    @pl.when(pl.program_id(0) == 0)
    def _():
        carry[...] = jnp.zeros_like(carry)

    h = _rms(x_ref[...], nw_ref[...]) * (1.0 + sc_ref[...]) + sh_ref[...]
    h_ref[...] = _pack_bf16_pairs(h)
    h_hi, h_lo = _split2(h)
    logits = _dot(h_hi, wrh_ref[...]) + _dot(h_hi, wrl_ref[...]) + _dot(h_lo, wrh_ref[...]) + br_ref[...]
    lane = lax.broadcasted_iota(jnp.int32, logits.shape, 1)
    vals, idxs = [], []
    rest = logits
    for _ in range(TOP_K):
        m = jnp.max(rest, axis=-1, keepdims=True)
        ix = jnp.min(jnp.where(rest == m, lane, LANES), axis=-1, keepdims=True)
        vals.append(m)
        idxs.append(ix)
        rest = jnp.where(lane == ix, -jnp.inf, rest)
    exps = [jnp.exp(vv - vals[0]) for vv in vals]
    den = exps[0]
    for e in exps[1:]:
        den = den + e
    hot = (lane == idxs[0]).astype(F32)
    for ix in idxs[1:]:
        hot = hot + (lane == ix).astype(F32)
    before = _dot(tri_ref[...], hot.astype(BF16)) + carry[0:1, :]
    carry[0:1, :] = carry[0:1, :] + jnp.sum(hot, axis=0, keepdims=True)
    idx_out = jnp.zeros(logits.shape, jnp.int32)
    gate_out = jnp.zeros(logits.shape, F32)
    rank_out = jnp.zeros(logits.shape, F32)
    for kk in range(TOP_K):
        rk = jnp.sum(jnp.where(lane == idxs[kk], before, 0.0), axis=-1, keepdims=True)
        idx_out = jnp.where(lane == kk, idxs[kk], idx_out)
        gate_out = jnp.where(lane == kk, exps[kk] / den, gate_out)
        rank_out = jnp.where(lane == kk, rk, rank_out)
    idx_ref[...] = idx_out
    gate_ref[...] = gate_out
    rank_ref[...] = rank_out.astype(jnp.int32)
    cnt_ref[...] = jnp.broadcast_to(carry[0:1, :], cnt_ref.shape).astype(jnp.int32)


def _router(x2d, nw, scale, shift, w_router, b_router, seq):
    t, d = x2d.shape
    e = w_router.shape[1]
    tm = min(256, seq)
    tps = seq // tm
    wr = jnp.zeros((d, LANES), F32).at[:, :e].set(w_router)
    wr_hi, wr_lo = _split2(wr)
    br = jnp.full((1, LANES), -1e30, F32).at[0, :e].set(b_router)
    tri = (jnp.arange(tm)[None, :] < jnp.arange(tm)[:, None]).astype(BF16)
    row_d = pl.BlockSpec((1, d), lambda i: (0, 0))
    mod = pl.BlockSpec((None, 1, d), lambda i: (i // tps, 0, 0))
    lane_out = pl.BlockSpec((tm, LANES), lambda i: (i, 0))
    return pl.pallas_call(
        _router_body,
        grid=(t // tm,),
        in_specs=[pl.BlockSpec((tm, d), lambda i: (i, 0)), row_d, mod, mod,
                  pl.BlockSpec((d, LANES), lambda i: (0, 0)),
                  pl.BlockSpec((d, LANES), lambda i: (0, 0)),
                  pl.BlockSpec((1, LANES), lambda i: (0, 0)),
                  pl.BlockSpec((tm, tm), lambda i: (0, 0))],
        out_specs=[pl.BlockSpec((tm, d // 2), lambda i: (i, 0)), lane_out, lane_out, lane_out,
                   pl.BlockSpec((8, LANES), lambda i: (0, 0))],
        out_shape=[jax.ShapeDtypeStruct((t, d // 2), jnp.int32),
                   jax.ShapeDtypeStruct((t, LANES), jnp.int32),
                   jax.ShapeDtypeStruct((t, LANES), F32),
                   jax.ShapeDtypeStruct((t, LANES), jnp.int32),
                   jax.ShapeDtypeStruct((8, LANES), jnp.int32)],
        scratch_shapes=[pltpu.VMEM((8, LANES), F32)],
        compiler_params=_cp("arbitrary"),
        name="norm2_router",
    )(x2d, nw, scale, shift, wr_hi, wr_lo, br, tri)


def _row_copy(src, dst, sem):
    return pltpu.make_async_copy(src, dst, sem)


def _dispatch_body(tm, dest_ref, h_ref, rows_in, rows_out, sem):
    del rows_in

    def issue(r, carry_):
        for kk in range(TOP_K):
            d = dest_ref[0, r * TOP_K + kk]
            _row_copy(h_ref.at[pl.ds(r, 1)], rows_out.at[pl.ds(d, 1)], sem).start()
        return carry_

    def drain(r, carry_):
        for _ in range(TOP_K):
            _row_copy(h_ref.at[pl.ds(0, 1)], rows_out.at[pl.ds(0, 1)], sem).wait()
        return carry_

    lax.fori_loop(0, tm, issue, 0)
    lax.fori_loop(0, tm, drain, 0)


def _dispatch(h2, dest, cap):
    t, d = h2.shape
    tm = min(128, t)
    dest3 = dest.reshape(t // tm, 1, tm * TOP_K)
    return pl.pallas_call(
        functools.partial(_dispatch_body, tm),
        grid=(t // tm,),
        in_specs=[pl.BlockSpec((None, 1, tm * TOP_K), lambda i: (i, 0, 0), memory_space=pltpu.SMEM),
                  pl.BlockSpec((tm, d), lambda i: (i, 0)),
                  pl.BlockSpec(memory_space=pl.ANY)],
        out_specs=pl.BlockSpec(memory_space=pl.ANY),
        out_shape=jax.ShapeDtypeStruct((cap, d), h2.dtype),
        scratch_shapes=[pltpu.SemaphoreType.DMA],
        input_output_aliases={2: 0},
        compiler_params=_cp("arbitrary"),
        name="moe_dispatch",
    )(dest3, h2, jnp.zeros((cap, d), h2.dtype))


def _moe_steps(blk_start, nblk, n_active, n_tiles, nb):
    n_experts = blk_start.shape[0]
    total = n_tiles * nb
    s = jnp.arange(total, dtype=jnp.int32)
    step_end = n_tiles * (blk_start + nblk)
    e = jnp.minimum(jnp.sum(step_end[None, :] <= s[:, None], axis=1), n_experts - 1).astype(jnp.int32)
    local = s - n_tiles * blk_start[e]
    per = jnp.maximum(nblk[e], 1)
    tile = local // per
    rb = blk_start[e] + local % per
    first = local % per == 0
    n_steps = (n_tiles * n_active).astype(jnp.int32)
    active = s < n_steps
    last = n_steps - 1
    spare = s - n_steps
    slot = (jnp.cumsum(first.astype(jnp.int32)) - 1) % 2
    nxt = jnp.minimum(s - local % per + per, total - 1)
    cols = (jnp.where(active, e, e[last]), jnp.where(active, tile, tile[last]), jnp.where(active, rb, rb[last]),
            jnp.where(active, tile, spare % n_tiles), jnp.where(active, rb, n_active + spare // n_tiles),
            first, slot, e[nxt], tile[nxt], nxt < n_steps)
    return tuple(x.astype(jnp.int32) for x in cols) + (n_steps.reshape(1),)


N_STEP_ARRAYS = 11


def _weight_stream(s, first, slot, e_cur, t_cur, e_nxt, t_nxt, has_nxt, copies, convert):
    @pl.when(first[s] == 1)
    def _():
        sl = slot[s]

        @pl.when(s == 0)
        def _():
            for cp in copies(e_cur[0], t_cur[0], 0):
                cp.start()

        for cp in copies(e_cur[s], t_cur[s], sl):
            cp.wait()
        convert(sl)

        @pl.when(has_nxt[s] == 1)
        def _():
            for cp in copies(e_nxt[s], t_nxt[s], 1 - sl):
                cp.start()


def _gate_up_body(se, st, sri, sto, sro, first, slot, en, tn_, hn, n_steps, x_ref, w_hbm, bg_ref, bu_ref,
                  act_ref, wf, wgb, wub, sem):
    s = pl.program_id(0)
    d, tn = wgb.shape
    nj = w_hbm.shape[2] // (2 * tn)

    @pl.when(s >= n_steps[0])
    def _():
        act_ref[...] = jnp.zeros_like(act_ref)

    @pl.when(s < n_steps[0])
    def _():
        def copies(e, j, sl):
            return [pltpu.make_async_copy(w_hbm.at[e, :, pl.ds(pl.multiple_of((m * nj + j) * tn, tn), tn)],
                                          wf.at[sl, m], sem.at[sl]) for m in range(2)]

        def convert(sl):
            wgb[...] = wf[sl, 0].astype(BF16)
            wub[...] = wf[sl, 1].astype(BF16)

        _weight_stream(s, first, slot, se, st, en, tn_, hn, copies, convert)
        xa, xb = _unpack_bf16_pairs(x_ref[...])
        half = d // 2

        def proj(w, b_ref):
            return _dot(xa, w[:half, :]) + _dot(xb, w[half:, :]) + b_ref[...]

        gate = jnp.minimum(proj(wgb, bg_ref), SWIGLU_LIMIT)
        up = jnp.clip(proj(wub, bu_ref), -SWIGLU_LIMIT, SWIGLU_LIMIT)
        act_ref[...] = ((up + 1.0) * gate * jax.nn.sigmoid(SWIGLU_ALPHA * gate)).astype(BF16)


def _gate_up(rows, steps, wgu, bgu):
    cap, half = rows.shape
    d = 2 * half
    e, _, two_de = wgu.shape
    de = two_de // 2
    bm = MOE_ROWS
    tn = min(MOE_UP_TILE, de)
    nj = de // tn
    grid_spec = pltpu.PrefetchScalarGridSpec(
        num_scalar_prefetch=N_STEP_ARRAYS,
        grid=(nj * (cap // bm),),
        in_specs=[pl.BlockSpec((bm, half), lambda s, se, st, sri, *_: (sri[s], 0)),
                  pl.BlockSpec(memory_space=pl.ANY),
                  pl.BlockSpec((None, 1, tn), lambda s, se, st, *_: (se[s], 0, st[s])),
                  pl.BlockSpec((None, 1, tn), lambda s, se, st, *_: (se[s], 0, nj + st[s]))],
        out_specs=pl.BlockSpec((bm, tn), lambda s, se, st, sri, sto, sro, *_: (sro[s], sto[s])),
        scratch_shapes=[pltpu.VMEM((2, 2, d, tn), F32), pltpu.VMEM((d, tn), BF16), pltpu.VMEM((d, tn), BF16),
                        pltpu.SemaphoreType.DMA((2,))],
    )
    bgu3 = bgu.reshape(e, 1, two_de)
    return pl.pallas_call(
        _gate_up_body,
        grid_spec=grid_spec,
        out_shape=jax.ShapeDtypeStruct((cap, de), BF16),
        compiler_params=_cp("arbitrary"),
        name="moe_gate_up",
    )(*steps, rows, wgu, bgu3, bgu3)


def _down_body(se, st, sri, sto, sro, first, slot, en, tn_, hn, n_steps, a_ref, w_hbm, bd_ref, o_ref,
               wf, wdb, sem):
    s = pl.program_id(0)
    tn = wdb.shape[1]

    @pl.when(s >= n_steps[0])
    def _():
        o_ref[...] = jnp.zeros_like(o_ref)

    @pl.when(s < n_steps[0])
    def _():
        def copies(e, j, sl):
            return [pltpu.make_async_copy(w_hbm.at[e, :, pl.ds(pl.multiple_of(j * tn, tn), tn)],
                                          wf.at[sl], sem.at[sl])]

        def convert(sl):
            wdb[...] = wf[sl].astype(BF16)

        _weight_stream(s, first, slot, se, st, en, tn_, hn, copies, convert)
        o_ref[...] = _dot(a_ref[...], wdb[...]) + bd_ref[...]


def _down(act, steps, wd, bd, tn):
    cap, de = act.shape
    e, _, d = wd.shape
    bm = MOE_ROWS
    grid_spec = pltpu.PrefetchScalarGridSpec(
        num_scalar_prefetch=N_STEP_ARRAYS,
        grid=((d // tn) * (cap // bm),),
        in_specs=[pl.BlockSpec((bm, de), lambda s, se, st, sri, *_: (sri[s], 0)),
                  pl.BlockSpec(memory_space=pl.ANY),
                  pl.BlockSpec((None, 1, tn), lambda s, se, st, *_: (se[s], 0, st[s]))],
        out_specs=pl.BlockSpec((bm, tn), lambda s, se, st, sri, sto, sro, *_: (sro[s], sto[s])),
        scratch_shapes=[pltpu.VMEM((2, de, tn), F32), pltpu.VMEM((de, tn), BF16), pltpu.SemaphoreType.DMA((2,))],
    )
    return pl.pallas_call(
        _down_body,
        grid_spec=grid_spec,
        out_shape=jax.ShapeDtypeStruct((cap, d), F32),
        compiler_params=_cp("arbitrary"),
        name="moe_down",
    )(*steps, act, wd, bd.reshape(e, 1, d))


def _combine_body(tm, n_tiles, dest_ref, dest_next_ref, rows_ref, gates_ref, x_ref, g2_ref, nw_ref, o_ref,
                  buf, sem):
    i = pl.program_id(0)
    slot = i % 2

    def fetch(dref, sl):
        def issue(r, carry_):
            for kk in range(TOP_K):
                d = dref[0, r * TOP_K + kk]
                _row_copy(rows_ref.at[pl.ds(d, 1)], buf.at[sl, kk, pl.ds(r, 1)], sem.at[sl]).start()
            return carry_

        lax.fori_loop(0, tm, issue, 0)

    @pl.when(i == 0)
    def _():
        fetch(dest_ref, 0)

    @pl.when(i + 1 < n_tiles)
    def _():
        fetch(dest_next_ref, 1 - slot)

    def drain(r, carry_):
        for kk in range(TOP_K):
            _row_copy(rows_ref.at[pl.ds(0, 1)], buf.at[slot, kk, pl.ds(0, 1)], sem.at[slot]).wait()
        return carry_

    lax.fori_loop(0, tm, drain, 0)
    gates = gates_ref[...]
    y = gates[:, 0:1] * buf[slot, 0]
    for kk in range(1, TOP_K):
        y = y + gates[:, kk:kk + 1] * buf[slot, kk]
    x2 = x_ref[...] + g2_ref[...] * y
    o_ref[...] = _rms(x2, nw_ref[...])


def _combine(out_rows, dest, gates, x1, gate2, normf_w, seq):
    t, d = x1.shape
    tm = min(64, seq)
    tps = seq // tm
    n_tiles = t // tm
    dest3 = dest.reshape(n_tiles, 1, tm * TOP_K)
    return pl.pallas_call(
        functools.partial(_combine_body, tm, n_tiles),
        grid=(n_tiles,),
        in_specs=[pl.BlockSpec((None, 1, tm * TOP_K), lambda i: (i, 0, 0), memory_space=pltpu.SMEM),
                  pl.BlockSpec((None, 1, tm * TOP_K), lambda i: (jnp.minimum(i + 1, n_tiles - 1), 0, 0),
                               memory_space=pltpu.SMEM),
                  pl.BlockSpec(memory_space=pl.ANY),
                  pl.BlockSpec((tm, LANES), lambda i: (i, 0)),
                  pl.BlockSpec((tm, d), lambda i: (i, 0)),
                  pl.BlockSpec((None, 1, d), lambda i: (i // tps, 0, 0)),
                  pl.BlockSpec((1, d), lambda i: (0, 0))],
        out_specs=pl.BlockSpec((tm, d), lambda i: (i, 0)),
        out_shape=jax.ShapeDtypeStruct((t, d), F32),
        scratch_shapes=[pltpu.VMEM((2, TOP_K, tm, d), F32), pltpu.SemaphoreType.DMA((2,))],
        compiler_params=_cp("arbitrary"),
        name="moe_combine",
    )(dest3, dest3, out_rows, gates, x1, gate2, normf_w.reshape(1, d))


def _moe_plan(idx, rank, counts, n_experts):
    bm = MOE_ROWS
    n_assign = idx.shape[0] * TOP_K
    nblk = (counts + bm - 1) // bm
    blk_end = jnp.cumsum(nblk)
    blk_start = blk_end - nblk
    dest = (bm * blk_start[idx] + rank).astype(jnp.int32)
    nb = (n_assign + n_experts * (bm - 1) + bm - 1) // bm
    return dest, blk_start.astype(jnp.int32), nblk.astype(jnp.int32), blk_end[-1].astype(jnp.int32), nb


def kernel(x, c, positions, w_ada, b_ada, norm1_w, w_in, rwkv_mu_rkvg, rwkv_mu_wa, rwkv_w0, rwkv_w1, rwkv_w2, rwkv_a0, rwkv_a1, rwkv_a2, rwkv_k_k, rwkv_k_a, rwkv_r_k, rwkv_lnx_w, rwkv_lnx_b, w_out, norm2_w, w_router, b_router, w_gate_up, b_gate_up, w_down, b_down, normf_w):
    bsz, seq, d = x.shape
    t = bsz * seq
    depth = w_ada.shape[0]
    rw = rwkv_w0.shape[1]
    ret_w = d - rw
    n_experts = w_router.shape[2]
    x2d = x.reshape(t, d)
    for l in range(depth):
        mod = _adaln(c, w_ada[l], b_ada[l])
        shift1, scale1, gate1, shift2, scale2, gate2 = [m.reshape(bsz, 1, d) for m in jnp.split(mod, 6, axis=-1)]
        h, tw, la = _norm1_lora(x2d, norm1_w[l].reshape(1, d), scale1, shift1,
                                rwkv_mu_wa[l, 0].reshape(1, d), rwkv_mu_wa[l, 1].reshape(1, d),
                                rwkv_w1[l].astype(BF16), rwkv_a1[l].astype(BF16), seq)
        p = _matmul(h, w_in[l].astype(BF16))
        mu = rwkv_mu_rkvg[l].reshape(4, 1, rw)
        prm = (mu[0], mu[1], mu[2], mu[3], rwkv_w0[l].reshape(1, rw), rwkv_a0[l].reshape(1, rw),
               rwkv_k_k[l].reshape(1, rw), rwkv_k_a[l].reshape(1, rw), rwkv_r_k[l].reshape(1, rw),
               rwkv_lnx_w[l].reshape(1, rw), rwkv_lnx_b[l].reshape(1, rw))
        y_rwkv = _rwkv(p, tw, la, prm, rwkv_w2[l].astype(BF16), rwkv_a2[l].astype(BF16), bsz, seq, rw)
        y_ret = _retention(p, positions, bsz, seq, 4 * rw, ret_w)
        wo = w_out[l].astype(BF16)
        x2d = _outproj(y_rwkv, y_ret, wo[:rw], wo[rw:], x2d, gate1, seq)
        h2, idx, gates, rank, counts = _router(x2d, norm2_w[l].reshape(1, d), scale2, shift2,
                                               w_router[l], b_router[l], seq)
        dest, blk_start, nblk, n_active, nb = _moe_plan(idx[:, :TOP_K], rank[:, :TOP_K], counts[0, :n_experts], n_experts)
        rows = _dispatch(h2, dest, nb * MOE_ROWS)
        de = w_down.shape[2]
        up_tiles = de // min(MOE_UP_TILE, de)
        act = _gate_up(rows, _moe_steps(blk_start, nblk, n_active, up_tiles, nb), w_gate_up[l], b_gate_up[l])
        tn_down = min(MOE_DOWN_TILE, d)
        out_rows = _down(act, _moe_steps(blk_start, nblk, n_active, d // tn_down, nb), w_down[l], b_down[l], tn_down)
        if l + 1 < depth:
            raise NotImplementedError("the final norm is fused into the last layer's combine")
        out = _combine(out_rows, dest, gates, x2d, gate2, normf_w, seq)
    return out.reshape(bsz, seq, d)
```

```python
import functools

import jax
import jax.numpy as jnp
from jax import lax
from jax.experimental import pallas as pl
from jax.experimental.pallas import tpu as pltpu

F32 = jnp.float32
BF16 = jnp.bfloat16

RWKV_HEAD = 64
RET_HEAD = 256
RET_CHUNK = 128
ROPE_BASE = 10000.0
TOP_K = 4
SWIGLU_LIMIT = 7.0
SWIGLU_ALPHA = 1.702
NORM_EPS = 1e-5
RWKV_GN_EPS = 64e-5
RET_GN_EPS = 1e-6

LANES = 128
SUBLANES = 8
RWKV_CHUNK = 64
RWKV_GROUP = 256
HEADS_PER_GROUP = RWKV_GROUP // RWKV_HEAD
MOE_ROWS = 512
MOE_UP_TILE = 256
MOE_DOWN_TILE = 2048
VMEM_LIMIT = 56 << 20


def _cp(*sem):
    return pltpu.CompilerParams(dimension_semantics=sem, vmem_limit_bytes=VMEM_LIMIT)


def _dot(a, b):
    return jnp.dot(a, b, preferred_element_type=F32)


def _dot_nt(a, b):
    return lax.dot_general(a, b, (((1,), (1,)), ((), ())), preferred_element_type=F32)


def _dot_tn(a, b):
    return lax.dot_general(a, b, (((0,), (0,)), ((), ())), preferred_element_type=F32)


def _split2(x):
    hi = x.astype(BF16)
    lo = (x - hi.astype(F32)).astype(BF16)
    return hi, lo


def _rms(x, w):
    return x * lax.rsqrt(jnp.mean(x * x, axis=-1, keepdims=True) + NORM_EPS) * w


def _adaln_body(c_ref, w_ref, b_ref, o_ref):
    c = c_ref[...]
    ca = c * jax.nn.sigmoid(c)
    o_ref[...] = _dot(ca.astype(BF16), w_ref[...].astype(BF16)) + b_ref[...]


def _adaln(c, w_ada, b_ada):
    bsz, d = c.shape
    n = w_ada.shape[1]
    tn = min(512, n)
    c8 = jnp.zeros((8, d), F32).at[:bsz].set(c)
    out = pl.pallas_call(
        _adaln_body,
        grid=(n // tn,),
        in_specs=[pl.BlockSpec((8, d), lambda j: (0, 0)),
                  pl.BlockSpec((d, tn), lambda j: (0, j)),
                  pl.BlockSpec((1, tn), lambda j: (0, j))],
        out_specs=pl.BlockSpec((8, tn), lambda j: (0, j)),
        out_shape=jax.ShapeDtypeStruct((8, n), F32),
        compiler_params=_cp("arbitrary"),
        name="adaln",
    )(c8, w_ada, b_ada.reshape(1, n))
    return out[:bsz]


def _norm1_body(tiles_per_seq, x_ref, nw_ref, sc_ref, sh_ref, muw_ref, mua_ref, w1_ref, a1_ref,
                h_ref, tw_ref, la_ref, carry):
    i = pl.program_id(0)

    @pl.when(i % tiles_per_seq == 0)
    def _():
        carry[...] = jnp.zeros_like(carry)

    x = x_ref[...]
    tm = x.shape[0]
    h = _rms(x, nw_ref[...]) * (1.0 + sc_ref[...]) + sh_ref[...]
    row = lax.broadcasted_iota(jnp.int32, h.shape, 0)
    h_prev = jnp.where(row == 0, carry[0:1, :], pltpu.roll(h, 1, 0))
    carry[0:1, :] = h[tm - 1:tm, :]
    dh = h_prev - h
    xw = h + dh * muw_ref[...]
    xa = h + dh * mua_ref[...]
    tw_ref[...] = jnp.tanh(_dot(xw.astype(BF16), w1_ref[...]))
    la_ref[...] = _dot(xa.astype(BF16), a1_ref[...])
    h_ref[...] = h.astype(BF16)


def _norm1_lora(x2d, nw, scale, shift, mu_w, mu_a, w1b, a1b, seq):
    t, d = x2d.shape
    tm = min(256, seq)
    tps = seq // tm
    r = w1b.shape[1]
    row_d = pl.BlockSpec((1, d), lambda i: (0, 0))
    mod = pl.BlockSpec((None, 1, d), lambda i: (i // tps, 0, 0))
    lora = pl.BlockSpec((d, r), lambda i: (0, 0))
    return pl.pallas_call(
        functools.partial(_norm1_body, tps),
        grid=(t // tm,),
        in_specs=[pl.BlockSpec((tm, d), lambda i: (i, 0)), row_d, mod, mod, row_d, row_d, lora, lora],
        out_specs=[pl.BlockSpec((tm, d), lambda i: (i, 0)),
                   pl.BlockSpec((tm, r), lambda i: (i, 0)),
                   pl.BlockSpec((tm, r), lambda i: (i, 0))],
        out_shape=[jax.ShapeDtypeStruct((t, d), BF16),
                   jax.ShapeDtypeStruct((t, r), F32),
                   jax.ShapeDtypeStruct((t, r), F32)],
        scratch_shapes=[pltpu.VMEM((8, d), F32)],
        compiler_params=_cp("arbitrary"),
        name="norm1_lora",
    )(x2d, nw, scale, shift, mu_w, mu_a, w1b, a1b)


def _matmul_body(a_ref, b_ref, o_ref):
    o_ref[...] = _dot(a_ref[...], b_ref[...])


def _matmul(a, b):
    m, k = a.shape
    n = b.shape[1]
    tm, tn = min(1024, m), min(512, n)
    return pl.pallas_call(
        _matmul_body,
        grid=(m // tm, n // tn),
        in_specs=[pl.BlockSpec((tm, k), lambda i, j: (i, 0)),
                  pl.BlockSpec((k, tn), lambda i, j: (0, j))],
        out_specs=pl.BlockSpec((tm, tn), lambda i, j: (i, j)),
        out_shape=jax.ShapeDtypeStruct((m, n), F32),
        compiler_params=_cp("parallel", "arbitrary"),
        name="in_proj",
    )(a, b)


def _rwkv_body(tb, pr_ref, pk_ref, pv_ref, pg_ref, tw_ref, la_ref,
               mur_ref, muk_ref, muv_ref, mug_ref, w0_ref, a0_ref, kk_ref, ka_ref, rk_ref,
               lnw_ref, lnb_ref, w2_ref, a2_ref,
               bd_ref, mstk_ref, sm_ref, im_ref, tri_ref, eye_ref, lvl_ref,
               y_ref,
               carry, state):
    i = pl.program_id(2)
    c = RWKV_CHUNK

    @pl.when(i == 0)
    def _():
        carry[...] = jnp.zeros_like(carry)
        state[...] = jnp.zeros_like(state)

    row = lax.broadcasted_iota(jnp.int32, (tb, RWKV_GROUP), 0)

    def shift_mix(ref, slot, mu):
        cur = ref[...]
        prev = jnp.where(row == 0, carry[slot:slot + 1, :], pltpu.roll(cur, 1, 0))
        carry[slot:slot + 1, :] = cur[tb - 1:tb, :]
        return cur + (prev - cur) * mu

    bd = bd_ref[...]

    def seg_sum(x):
        hi, lo = _split2(x)
        return _dot(hi, bd) + _dot(lo, bd)

    r = shift_mix(pr_ref, 0, mur_ref[...])
    k = shift_mix(pk_ref, 1, muk_ref[...])
    v = shift_mix(pv_ref, 2, muv_ref[...])
    g = shift_mix(pg_ref, 3, mug_ref[...])

    z = -(w0_ref[...] + _dot(tw_ref[...].astype(BF16), w2_ref[...]))
    softplus = jnp.maximum(z, 0.0) + jnp.log(1.0 + jnp.exp(-jnp.abs(z)))
    lw = -jnp.exp(-softplus - 0.5)
    a_icl = jax.nn.sigmoid(a0_ref[...] + _dot(la_ref[...].astype(BF16), a2_ref[...]))
    kk = k * kk_ref[...]
    kk = kk / jnp.maximum(jnp.sqrt(seg_sum(kk * kk)), 1e-12)
    k = k * (1.0 + (a_icl - 1.0) * ka_ref[...])
    a_in = -kk
    b_in = kk * a_icl

    mstk = mstk_ref[...]
    strict = sm_ref[...] > 0.0
    incl = im_ref[...] > 0.0
    same_head = bd > 0
    tri = tri_ref[...]

    def stack(x):
        return jnp.concatenate([x.astype(BF16)] * HEADS_PER_GROUP, axis=0) * mstk

    n_ch = tb // c

    def each(fn, *cols):
        return [fn(*args) for args in zip(*cols)]

    rows = [slice(ci * c, (ci + 1) * c) for ci in range(n_ch)]
    lw_c = [lw[sl] for sl in rows]

    def cumsum(x):
        hi, lo = _split2(x)
        return _dot(tri, hi) + _dot(tri, lo)

    cum = each(cumsum, lw_c)
    cum_end = [x[c - 1:c, :] for x in cum]
    p_inv = each(lambda x: jnp.exp(-x), cum)
    p_end = each(lambda x, e: jnp.exp(e - x), cum, cum_end)
    a_d = each(lambda sl, x, l: (a_in[sl] * jnp.exp(x - l)).astype(BF16), rows, cum, lw_c)
    r_d = each(lambda sl, x: r[sl] * jnp.exp(x), rows, cum)
    r_b = each(lambda x: x.astype(BF16), r_d)
    a_t = each(stack, a_d)
    b_t = each(lambda sl, p: stack(b_in[sl] * p), rows, p_inv)
    k_t = each(lambda sl, p: stack(k[sl] * p), rows, p_inv)
    v_t = each(lambda sl: stack(v[sl]), rows)
    b_h = each(lambda sl, p: (b_in[sl] * p).astype(BF16), rows, p_end)
    uk_r = each(lambda sl, p: (k[sl] * p).astype(BF16), rows, p_end)
    a_ab = each(lambda x, y: jnp.where(strict, _dot_nt(x, y), 0.0), a_d, b_t)
    a_ak = each(lambda x, y: jnp.where(strict, _dot_nt(x, y), 0.0).astype(BF16), a_d, k_t)
    a_rb = each(lambda x, y: jnp.where(incl, _dot_nt(x, y), 0.0).astype(BF16), r_b, b_t)
    a_rk = each(lambda x, y: jnp.where(incl, _dot_nt(x, y), 0.0).astype(BF16), r_b, k_t)
    inv = each(lambda x: eye_ref[...] + x * lvl_ref[0], a_ab)
    for lv in range(1, c.bit_length() - 1):
        inv_bd = each(stack, inv)
        w = each(lambda x, t: _dot((x * lvl_ref[lv]).astype(BF16), t), a_ab, inv_bd)
        inv = each(lambda x, ww: x + _dot(x.astype(BF16), stack(ww)), inv, w)
    invb = each(lambda x: x.astype(BF16), inv)
    a_p = each(_dot, invb, a_t)
    akv = each(_dot, a_ak, v_t)
    u_loc = each(lambda t, x: _dot(t, stack(x)), invb, akv)
    r_p = each(lambda x, m, y: (x + _dot(m, stack(y))).astype(BF16), r_d, a_rb, a_p)
    y_loc = each(lambda m, x, n, y: _dot(m, stack(x)) + _dot(n, y), a_rb, u_loc, a_rk, v_t)
    phi = each(lambda x, y: jnp.where(same_head, _dot_tn(x.astype(BF16), y), 0.0).astype(BF16), a_p, b_h)
    s_loc = each(lambda x, y, sl, z: jnp.where(
        same_head,
        _dot_tn(jnp.concatenate([x.astype(BF16), v[sl].astype(BF16)], axis=0), jnp.concatenate([y, z], axis=0)),
        0.0), u_loc, b_h, rows, uk_r)

    s = state[...]
    ys = []
    for ci in range(n_ch):
        sb = s.astype(BF16)
        ys.append(_dot_nt(r_p[ci], sb) + y_loc[ci])
        s = s * jnp.exp(cum_end[ci]) + _dot(sb, phi[ci]) + s_loc[ci]
    state[...] = s

    y = jnp.concatenate(ys, axis=0)
    inv_n = 1.0 / RWKV_HEAD
    mean = seg_sum(y) * inv_n
    dlt = y - mean
    var = seg_sum(dlt * dlt) * inv_n
    yn = dlt * lax.rsqrt(var + RWKV_GN_EPS) * lnw_ref[...] + lnb_ref[...]
    bonus = seg_sum(r * k * rk_ref[...]) * v
    y_ref[...] = ((yn + bonus) * jax.nn.sigmoid(g)).astype(BF16)


def _rwkv_consts():
    n = RWKV_GROUP
    c = RWKV_CHUNK
    lane = jnp.arange(n)
    head = lane // RWKV_HEAD
    blk = lane // c
    bd = (head[:, None] == head[None, :]).astype(BF16)
    mstk = (blk[:, None] == head[None, :]).astype(BF16)
    t_idx = jnp.arange(c)[:, None]
    s_idx = (lane % c)[None, :]
    strict = (s_idx < t_idx).astype(F32)
    incl = (s_idx <= t_idx).astype(F32)
    eye = (s_idx == t_idx).astype(F32)
    tri = (jnp.arange(c)[None, :] <= jnp.arange(c)[:, None]).astype(BF16)
    lvls = []
    for lv in range(c.bit_length() - 1):
        m = 1 << lv
        hi_lo = ((t_idx // m) % 2 == 1) & ((s_idx // m) % 2 == 0)
        lvls.append((hi_lo & (t_idx // (2 * m) == s_idx // (2 * m))).astype(F32))
    return bd, mstk, strict, incl, tri, eye, jnp.stack(lvls)


def _rwkv(p, tw, la, prm, w2b, a2b, bsz, seq, rw):
    t = p.shape[0]
    g = rw // RWKV_GROUP
    tb = min(512, seq)
    nt = seq // tb
    r = tw.shape[1]
    bd, mstk, strict, incl, tri, eye, lvls = _rwkv_consts()

    def pcol(m):
        return pl.BlockSpec((tb, RWKV_GROUP), lambda b, gg, i: (b * nt + i, m * g + gg))

    lor = pl.BlockSpec((tb, r), lambda b, gg, i: (b * nt + i, 0))
    chan = pl.BlockSpec((1, RWKV_GROUP), lambda b, gg, i: (0, gg))
    lw2 = pl.BlockSpec((r, RWKV_GROUP), lambda b, gg, i: (0, gg))
    sq = pl.BlockSpec((RWKV_GROUP, RWKV_GROUP), lambda b, gg, i: (0, 0))
    tri_spec = pl.BlockSpec((RWKV_CHUNK, RWKV_CHUNK), lambda b, gg, i: (0, 0))
    wide = pl.BlockSpec((RWKV_CHUNK, RWKV_GROUP), lambda b, gg, i: (0, 0))
    return pl.pallas_call(
        functools.partial(_rwkv_body, tb),
        grid=(bsz, g, nt),
        in_specs=[pcol(0), pcol(1), pcol(2), pcol(3), lor, lor] + [chan] * 11 + [lw2, lw2]
                 + [sq, sq, wide, wide, tri_spec, wide, pl.BlockSpec(lvls.shape, lambda b, gg, i: (0, 0, 0))],
        out_specs=pl.BlockSpec((tb, RWKV_GROUP), lambda b, gg, i: (b * nt + i, gg)),
        out_shape=jax.ShapeDtypeStruct((t, rw), BF16),
        scratch_shapes=[pltpu.VMEM((8, RWKV_GROUP), F32), pltpu.VMEM((RWKV_GROUP, RWKV_GROUP), F32)],
        compiler_params=_cp("parallel", "parallel", "arbitrary"),
        name="rwkv7",
    )(p, p, p, p, tw, la, *prm, w2b, a2b, bd, mstk, strict, incl, tri, eye, lvls)


def _ret_body(heads, q_ref, k_ref, v_ref, g_ref, pos_ref, invf_ref, inner_ref, qd_ref, kd_ref, cd_ref,
              o_ref, state):
    @pl.when(pl.program_id(1) == 0)
    def _():
        state[...] = jnp.zeros_like(state)

    half = RET_HEAD // 2
    ang = pos_ref[...].astype(F32) * invf_ref[...]
    cos = jnp.cos(ang)
    sin = jnp.sin(ang)

    def rope(x):
        x1, x2 = x[:, :half], x[:, half:]
        return jnp.concatenate([x1 * cos - x2 * sin, x1 * sin + x2 * cos], axis=-1)

    for hh in range(heads):
        sl = slice(hh * RET_HEAD, (hh + 1) * RET_HEAD)
        q = rope(q_ref[:, sl])
        k = rope(k_ref[:, sl]) * (RET_HEAD ** -0.5)
        v = v_ref[:, sl].astype(BF16)
        g = g_ref[:, sl]
        qb = q.astype(BF16)
        scores = _dot_nt(qb, k.astype(BF16)) * inner_ref[hh]
        st = state[hh]
        o = _dot(scores.astype(BF16), v) + _dot(qb, st.astype(BF16)) * qd_ref[hh]
        state[hh] = st * cd_ref[hh] + _dot_tn((k * kd_ref[hh]).astype(BF16), v)
        on = o * lax.rsqrt(jnp.mean(o * o, axis=-1, keepdims=True) + RET_GN_EPS)
        o_ref[:, sl] = (on * (g * jax.nn.sigmoid(g))).astype(BF16)


def _retention(p, positions, bsz, seq, col0, width):
    t = p.shape[0]
    heads = width // RET_HEAD
    c = RET_CHUNK
    nc = seq // c
    half = RET_HEAD // 2
    cb = col0 // width
    inv_freq = (ROPE_BASE ** (-jnp.arange(half, dtype=F32) / half)).reshape(1, half)
    log_gamma = jnp.log(1.0 - 2.0 ** (-5.0 - jnp.arange(heads, dtype=F32)))
    idx = jnp.arange(c, dtype=F32)
    diff = idx[:, None] - idx[None, :]
    inner = jnp.where(diff >= 0, jnp.exp(log_gamma[:, None, None] * jnp.maximum(diff, 0.0)), 0.0)
    q_decay = jnp.exp(log_gamma[:, None] * (idx + 1.0))[:, :, None]
    k_decay = jnp.exp(log_gamma[:, None] * (c - 1.0 - idx))[:, :, None]
    chunk_decay = jnp.broadcast_to(jnp.exp(log_gamma * c)[:, None, None], (heads, 1, RET_HEAD))

    def pcol(m):
        return pl.BlockSpec((c, width), lambda b, i: (b * nc + i, cb + m))

    def full(a):
        return pl.BlockSpec(a.shape, lambda b, i: (0,) * a.ndim)

    consts = (inv_freq, inner, q_decay, k_decay, chunk_decay)
    return pl.pallas_call(
        functools.partial(_ret_body, heads),
        grid=(bsz, nc),
        in_specs=[pcol(0), pcol(1), pcol(2), pcol(3),
                  pl.BlockSpec((c, 1), lambda b, i: (b * nc + i, 0))] + [full(a) for a in consts],
        out_specs=pl.BlockSpec((c, width), lambda b, i: (b * nc + i, 0)),
        out_shape=jax.ShapeDtypeStruct((t, width), BF16),
        scratch_shapes=[pltpu.VMEM((heads, RET_HEAD, RET_HEAD), F32)],
        compiler_params=_cp("parallel", "arbitrary"),
        name="retention",
    )(p, p, p, p, positions.reshape(t, 1), *consts)


def _outproj_body(ya_ref, yb_ref, wa_ref, wb_ref, x_ref, g_ref, o_ref):
    mix = _dot(ya_ref[...], wa_ref[...]) + _dot(yb_ref[...], wb_ref[...])
    o_ref[...] = x_ref[...] + g_ref[...] * mix


def _outproj(ya, yb, wa, wb, x2d, gate, seq):
    t, d = x2d.shape
    ka, kb = ya.shape[1], yb.shape[1]
    tm, tn = min(1024, seq), min(512, d)
    tps = seq // tm
    return pl.pallas_call(
        _outproj_body,
        grid=(t // tm, d // tn),
        in_specs=[pl.BlockSpec((tm, ka), lambda i, j: (i, 0)),
                  pl.BlockSpec((tm, kb), lambda i, j: (i, 0)),
                  pl.BlockSpec((ka, tn), lambda i, j: (0, j)),
                  pl.BlockSpec((kb, tn), lambda i, j: (0, j)),
                  pl.BlockSpec((tm, tn), lambda i, j: (i, j)),
                  pl.BlockSpec((None, 1, tn), lambda i, j: (i // tps, 0, j))],
        out_specs=pl.BlockSpec((tm, tn), lambda i, j: (i, j)),
        out_shape=jax.ShapeDtypeStruct((t, d), F32),
        compiler_params=_cp("parallel", "arbitrary"),
        name="out_proj",
    )(ya, yb, wa, wb, x2d, gate)


def _pack_bf16_pairs(h):
    half = h.shape[1] // 2
    hi = lax.bitcast_convert_type(h[:, :half].astype(BF16).astype(F32), jnp.int32)
    lo = lax.bitcast_convert_type(h[:, half:].astype(BF16).astype(F32), jnp.int32)
    return hi | lax.shift_right_logical(lo, 16)


def _unpack_bf16_pairs(w):
    hi = lax.bitcast_convert_type(w & -65536, F32).astype(BF16)
    lo = lax.bitcast_convert_type(w << 16, F32).astype(BF16)
    return hi, lo


def _router_body(x_ref, nw_ref, sc_ref, sh_ref, wrh_ref, wrl_ref, br_ref, tri_ref,
                 h_ref, idx_ref, gate_ref, rank_ref, cnt_ref, carry):
    @pl.when(pl.program_id(0) == 0)
    def _():
        carry[...] = jnp.zeros_like(carry)

    h = _rms(x_ref[...], nw_ref[...]) * (1.0 + sc_ref[...]) + sh_ref[...]
    h_ref[...] = _pack_bf16_pairs(h)
    h_hi, h_lo = _split2(h)
    logits = _dot(h_hi, wrh_ref[...]) + _dot(h_hi, wrl_ref[...]) + _dot(h_lo, wrh_ref[...]) + br_ref[...]
    lane = lax.broadcasted_iota(jnp.int32, logits.shape, 1)
    vals, idxs = [], []
    rest = logits
    for _ in range(TOP_K):
        m = jnp.max(rest, axis=-1, keepdims=True)
        ix = jnp.min(jnp.where(rest == m, lane, LANES), axis=-1, keepdims=True)
        vals.append(m)
        idxs.append(ix)
        rest = jnp.where(lane == ix, -jnp.inf, rest)
    exps = [jnp.exp(vv - vals[0]) for vv in vals]
    den = exps[0]
    for e in exps[1:]:
        den = den + e
    hot = (lane == idxs[0]).astype(F32)
    for ix in idxs[1:]:
        hot = hot + (lane == ix).astype(F32)
    before = _dot(tri_ref[...], hot.astype(BF16)) + carry[0:1, :]
    carry[0:1, :] = carry[0:1, :] + jnp.sum(hot, axis=0, keepdims=True)
    idx_out = jnp.zeros(logits.shape, jnp.int32)
    gate_out = jnp.zeros(logits.shape, F32)
    rank_out = jnp.zeros(logits.shape, F32)
    for kk in range(TOP_K):
        rk = jnp.sum(jnp.where(lane == idxs[kk], before, 0.0), axis=-1, keepdims=True)
        idx_out = jnp.where(lane == kk, idxs[kk], idx_out)
        gate_out = jnp.where(lane == kk, exps[kk] / den, gate_out)
        rank_out = jnp.where(lane == kk, rk, rank_out)
    idx_ref[...] = idx_out
    gate_ref[...] = gate_out
    rank_ref[...] = rank_out.astype(jnp.int32)
    cnt_ref[...] = jnp.broadcast_to(carry[0:1, :], cnt_ref.shape).astype(jnp.int32)


def _router(x2d, nw, scale, shift, w_router, b_router, seq):
    t, d = x2d.shape
    e = w_router.shape[1]
    tm = min(256, seq)
    tps = seq // tm
    wr = jnp.zeros((d, LANES), F32).at[:, :e].set(w_router)
    wr_hi, wr_lo = _split2(wr)
    br = jnp.full((1, LANES), -1e30, F32).at[0, :e].set(b_router)
    tri = (jnp.arange(tm)[None, :] < jnp.arange(tm)[:, None]).astype(BF16)
    row_d = pl.BlockSpec((1, d), lambda i: (0, 0))
    mod = pl.BlockSpec((None, 1, d), lambda i: (i // tps, 0, 0))
    lane_out = pl.BlockSpec((tm, LANES), lambda i: (i, 0))
    return pl.pallas_call(
        _router_body,
        grid=(t // tm,),
        in_specs=[pl.BlockSpec((tm, d), lambda i: (i, 0)), row_d, mod, mod,
                  pl.BlockSpec((d, LANES), lambda i: (0, 0)),
                  pl.BlockSpec((d, LANES), lambda i: (0, 0)),
                  pl.BlockSpec((1, LANES), lambda i: (0, 0)),
                  pl.BlockSpec((tm, tm), lambda i: (0, 0))],
        out_specs=[pl.BlockSpec((tm, d // 2), lambda i: (i, 0)), lane_out, lane_out, lane_out,
                   pl.BlockSpec((8, LANES), lambda i: (0, 0))],
        out_shape=[jax.ShapeDtypeStruct((t, d // 2), jnp.int32),
                   jax.ShapeDtypeStruct((t, LANES), jnp.int32),
                   jax.ShapeDtypeStruct((t, LANES), F32),
                   jax.ShapeDtypeStruct((t, LANES), jnp.int32),
                   jax.ShapeDtypeStruct((8, LANES), jnp.int32)],
        scratch_shapes=[pltpu.VMEM((8, LANES), F32)],
        compiler_params=_cp("arbitrary"),
        name="norm2_router",
    )(x2d, nw, scale, shift, wr_hi, wr_lo, br, tri)


def _row_copy(src, dst, sem):
    return pltpu.make_async_copy(src, dst, sem)


def _dispatch_body(tm, gaps_ref, dest_ref, h_ref, rows_out, zeros, sem, zsem):
    zr, half = zeros.shape
    n_experts = gaps_ref.shape[1]

    def zero_fill(start_or_wait):
        def gap(e, carry_):
            pos = gaps_ref[0, e]
            length = gaps_ref[1, e]
            head = jnp.minimum((-pos) % SUBLANES, length)
            n_tiles = (length - head) // SUBLANES
            body = pos + head
            tail = body + n_tiles * SUBLANES

            def row(at):
                def one(r, c2):
                    start_or_wait(pltpu.make_async_copy(zeros.at[pl.ds(0, 1)], rows_out.at[pl.ds(at + r, 1)], zsem))
                    return c2
                return one

            def tile(r, c2):
                dst = rows_out.at[pl.ds(pl.multiple_of(body + r * SUBLANES, SUBLANES), SUBLANES)]
                start_or_wait(pltpu.make_async_copy(zeros.at[pl.ds(0, SUBLANES)], dst, zsem))
                return c2

            lax.fori_loop(0, head, row(pos), 0)
            lax.fori_loop(0, n_tiles, tile, 0)
            lax.fori_loop(0, length - head - n_tiles * SUBLANES, row(tail), 0)
            return carry_

        lax.fori_loop(0, n_experts, gap, 0)

        def spare(b, carry_):
            for hh in range(MOE_ROWS // zr):
                dst = rows_out.at[pl.ds(pl.multiple_of(b * MOE_ROWS + hh * zr, zr), zr)]
                start_or_wait(pltpu.make_async_copy(zeros, dst, zsem))
            return carry_

        lax.fori_loop(gaps_ref[2, 0], rows_out.shape[0] // MOE_ROWS, spare, 0)

    @pl.when(pl.program_id(0) == 0)
    def _():
        zeros[...] = jnp.zeros_like(zeros)
        zero_fill(lambda cp: cp.start())

    def issue(r, carry_):
        for kk in range(TOP_K):
            d = dest_ref[0, r * TOP_K + kk]
            _row_copy(h_ref.at[pl.ds(r, 1)], rows_out.at[pl.ds(d, 1)], sem).start()
        return carry_

    def drain(r, carry_):
        for _ in range(TOP_K):
            _row_copy(h_ref.at[pl.ds(0, 1)], rows_out.at[pl.ds(0, 1)], sem).wait()
        return carry_

    lax.fori_loop(0, tm, issue, 0)
    lax.fori_loop(0, tm, drain, 0)

    @pl.when(pl.program_id(0) == 0)
    def _():
        zero_fill(lambda cp: cp.wait())


def _dispatch(h2, dest, gaps, cap):
    t, d = h2.shape
    tm = min(128, t)
    dest3 = dest.reshape(t // tm, 1, tm * TOP_K)
    return pl.pallas_call(
        functools.partial(_dispatch_body, tm),
        grid=(t // tm,),
        in_specs=[pl.BlockSpec(memory_space=pltpu.SMEM),
                  pl.BlockSpec((None, 1, tm * TOP_K), lambda i: (i, 0, 0), memory_space=pltpu.SMEM),
                  pl.BlockSpec((tm, d), lambda i: (i, 0))],
        out_specs=pl.BlockSpec(memory_space=pl.ANY),
        out_shape=jax.ShapeDtypeStruct((cap, d), h2.dtype),
        scratch_shapes=[pltpu.VMEM((MOE_ROWS // 2, d), h2.dtype), pltpu.SemaphoreType.DMA, pltpu.SemaphoreType.DMA],
        compiler_params=_cp("arbitrary"),
        name="moe_dispatch",
    )(gaps, dest3, h2)


def _moe_steps(counts, blk_start, nblk, n_active, n_tiles, nb):
    n_experts = blk_start.shape[0]
    total = n_tiles * nb
    s = jnp.arange(total, dtype=jnp.int32)
    step_end = n_tiles * (blk_start + nblk)
    e = jnp.minimum(jnp.sum(step_end[None, :] <= s[:, None], axis=1), n_experts - 1).astype(jnp.int32)
    local = s - n_tiles * blk_start[e]
    per = jnp.maximum(nblk[e], 1)
    tile = local // per
    rb = blk_start[e] + local % per
    first = local % per == 0
    n_steps = (n_tiles * n_active).astype(jnp.int32)
    active = s < n_steps
    last = n_steps - 1
    spare = s - n_steps
    slot = (jnp.cumsum(first.astype(jnp.int32)) - 1) % 2
    nxt = jnp.minimum(s - local % per + per, total - 1)
    cols = (jnp.where(active, e, e[last]), jnp.where(active, tile, tile[last]), jnp.where(active, rb, rb[last]),
            jnp.where(active, tile, spare % n_tiles), jnp.where(active, rb, n_active + spare // n_tiles),
            first, slot, e[nxt], tile[nxt], nxt < n_steps,
            counts[e] - MOE_ROWS * (local % per) <= MOE_ROWS // 2)
    return tuple(x.astype(jnp.int32) for x in cols) + (n_steps.reshape(1),)


N_STEP_ARRAYS = 12


def _weight_stream(s, first, slot, e_cur, t_cur, e_nxt, t_nxt, has_nxt, copies, convert):
    @pl.when(first[s] == 1)
    def _():
        sl = slot[s]

        @pl.when(s == 0)
        def _():
            for cp in copies(e_cur[0], t_cur[0], 0):
                cp.start()

        for cp in copies(e_cur[s], t_cur[s], sl):
            cp.wait()
        convert(sl)

        @pl.when(has_nxt[s] == 1)
        def _():
            for cp in copies(e_nxt[s], t_nxt[s], 1 - sl):
                cp.start()


def _gate_up_body(se, st, sri, sto, sro, first, slot, en, tn_, hn, half_blk, n_steps, x_ref, w_hbm, bg_ref, bu_ref,
                  act_ref, wf, wgb, wub, sem):
    s = pl.program_id(0)
    d, tn = wgb.shape
    nj = w_hbm.shape[2] // (2 * tn)
    bm = x_ref.shape[0]

    @pl.when(s >= n_steps[0])
    def _():
        act_ref[...] = jnp.zeros_like(act_ref)

    @pl.when(s < n_steps[0])
    def _():
        def copies(e, j, sl):
            return [pltpu.make_async_copy(w_hbm.at[e, :, pl.ds(pl.multiple_of((m * nj + j) * tn, tn), tn)],
                                          wf.at[sl, m], sem.at[sl]) for m in range(2)]

        def convert(sl):
            wgb[...] = wf[sl, 0].astype(BF16)
            wub[...] = wf[sl, 1].astype(BF16)

        _weight_stream(s, first, slot, se, st, en, tn_, hn, copies, convert)
        half = d // 2

        def rows(n):
            xa, xb = _unpack_bf16_pairs(x_ref[0:n, :])

            def proj(w, b_ref):
                return _dot(xa, w[:half, :]) + _dot(xb, w[half:, :]) + b_ref[...]

            gate = jnp.minimum(proj(wgb, bg_ref), SWIGLU_LIMIT)
            up = jnp.clip(proj(wub, bu_ref), -SWIGLU_LIMIT, SWIGLU_LIMIT)
            act_ref[0:n, :] = ((up + 1.0) * gate * jax.nn.sigmoid(SWIGLU_ALPHA * gate)).astype(BF16)

        @pl.when(half_blk[s] == 0)
        def _():
            rows(bm)

        @pl.when(half_blk[s] == 1)
        def _():
            rows(bm // 2)
            act_ref[bm // 2:, :] = jnp.zeros((bm - bm // 2, tn), BF16)


def _gate_up(rows, steps, wgu, bgu):
    cap, half = rows.shape
    d = 2 * half
    e, _, two_de = wgu.shape
    de = two_de // 2
    bm = MOE_ROWS
    tn = min(MOE_UP_TILE, de)
    nj = de // tn
    grid_spec = pltpu.PrefetchScalarGridSpec(
        num_scalar_prefetch=N_STEP_ARRAYS,
        grid=(nj * (cap // bm),),
        in_specs=[pl.BlockSpec((bm, half), lambda s, se, st, sri, *_: (sri[s], 0)),
                  pl.BlockSpec(memory_space=pl.ANY),
                  pl.BlockSpec((None, 1, tn), lambda s, se, st, *_: (se[s], 0, st[s])),
                  pl.BlockSpec((None, 1, tn), lambda s, se, st, *_: (se[s], 0, nj + st[s]))],
        out_specs=pl.BlockSpec((bm, tn), lambda s, se, st, sri, sto, sro, *_: (sro[s], sto[s])),
        scratch_shapes=[pltpu.VMEM((2, 2, d, tn), F32), pltpu.VMEM((d, tn), BF16), pltpu.VMEM((d, tn), BF16),
                        pltpu.SemaphoreType.DMA((2,))],
    )
    bgu3 = bgu.reshape(e, 1, two_de)
    return pl.pallas_call(
        _gate_up_body,
        grid_spec=grid_spec,
        out_shape=jax.ShapeDtypeStruct((cap, de), BF16),
        compiler_params=_cp("arbitrary"),
        name="moe_gate_up",
    )(*steps, rows, wgu, bgu3, bgu3)


def _down_body(se, st, sri, sto, sro, first, slot, en, tn_, hn, half_blk, n_steps, a_ref, w_hbm, bd_ref, o_ref,
               wf, wdb, sem):
    s = pl.program_id(0)
    tn = wdb.shape[1]
    bm = a_ref.shape[0]

    @pl.when(s >= n_steps[0])
    def _():
        o_ref[...] = jnp.zeros_like(o_ref)

    @pl.when(s < n_steps[0])
    def _():
        def copies(e, j, sl):
            return [pltpu.make_async_copy(w_hbm.at[e, :, pl.ds(pl.multiple_of(j * tn, tn), tn)],
                                          wf.at[sl], sem.at[sl])]

        def convert(sl):
            wdb[...] = wf[sl].astype(BF16)

        _weight_stream(s, first, slot, se, st, en, tn_, hn, copies, convert)

        @pl.when(half_blk[s] == 0)
        def _():
            o_ref[...] = _dot(a_ref[...], wdb[...]) + bd_ref[...]

        @pl.when(half_blk[s] == 1)
        def _():
            o_ref[0:bm // 2, :] = _dot(a_ref[0:bm // 2, :], wdb[...]) + bd_ref[...]
            o_ref[bm // 2:, :] = jnp.zeros((bm - bm // 2, tn), F32)


def _down(act, steps, wd, bd, tn):
    cap, de = act.shape
    e, _, d = wd.shape
    bm = MOE_ROWS
    grid_spec = pltpu.PrefetchScalarGridSpec(
        num_scalar_prefetch=N_STEP_ARRAYS,
        grid=((d // tn) * (cap // bm),),
        in_specs=[pl.BlockSpec((bm, de), lambda s, se, st, sri, *_: (sri[s], 0)),
                  pl.BlockSpec(memory_space=pl.ANY),
                  pl.BlockSpec((None, 1, tn), lambda s, se, st, *_: (se[s], 0, st[s]))],
        out_specs=pl.BlockSpec((bm, tn), lambda s, se, st, sri, sto, sro, *_: (sro[s], sto[s])),
        scratch_shapes=[pltpu.VMEM((2, de, tn), F32), pltpu.VMEM((de, tn), BF16), pltpu.SemaphoreType.DMA((2,))],
    )
    return pl.pallas_call(
        _down_body,
        grid_spec=grid_spec,
        out_shape=jax.ShapeDtypeStruct((cap, d), F32),
        compiler_params=_cp("arbitrary"),
        name="moe_down",
    )(*steps, act, wd, bd.reshape(e, 1, d))


def _combine_body(tm, n_tiles, dest_ref, dest_next_ref, rows_ref, gates_ref, x_ref, g2_ref, nw_ref, o_ref,
                  buf, sem):
    i = pl.program_id(0)
    slot = i % 2

    def fetch(dref, sl):
        def issue(r, carry_):
            for kk in range(TOP_K):
                d = dref[0, r * TOP_K + kk]
                _row_copy(rows_ref.at[pl.ds(d, 1)], buf.at[sl, kk, pl.ds(r, 1)], sem.at[sl]).start()
            return carry_

        lax.fori_loop(0, tm, issue, 0)

    @pl.when(i == 0)
    def _():
        fetch(dest_ref, 0)

    @pl.when(i + 1 < n_tiles)
    def _():
        fetch(dest_next_ref, 1 - slot)

    def drain(r, carry_):
        for kk in range(TOP_K):
            _row_copy(rows_ref.at[pl.ds(0, 1)], buf.at[slot, kk, pl.ds(0, 1)], sem.at[slot]).wait()
        return carry_

    lax.fori_loop(0, tm, drain, 0)
    gates = gates_ref[...]
    y = gates[:, 0:1] * buf[slot, 0]
    for kk in range(1, TOP_K):
        y = y + gates[:, kk:kk + 1] * buf[slot, kk]
    x2 = x_ref[...] + g2_ref[...] * y
    o_ref[...] = _rms(x2, nw_ref[...])


def _combine(out_rows, dest, gates, x1, gate2, normf_w, seq):
    t, d = x1.shape
    tm = min(64, seq)
    tps = seq // tm
    n_tiles = t // tm
    dest3 = dest.reshape(n_tiles, 1, tm * TOP_K)
    return pl.pallas_call(
        functools.partial(_combine_body, tm, n_tiles),
        grid=(n_tiles,),
        in_specs=[pl.BlockSpec((None, 1, tm * TOP_K), lambda i: (i, 0, 0), memory_space=pltpu.SMEM),
                  pl.BlockSpec((None, 1, tm * TOP_K), lambda i: (jnp.minimum(i + 1, n_tiles - 1), 0, 0),
                               memory_space=pltpu.SMEM),
                  pl.BlockSpec(memory_space=pl.ANY),
                  pl.BlockSpec((tm, LANES), lambda i: (i, 0)),
                  pl.BlockSpec((tm, d), lambda i: (i, 0)),
                  pl.BlockSpec((None, 1, d), lambda i: (i // tps, 0, 0)),
                  pl.BlockSpec((1, d), lambda i: (0, 0))],
        out_specs=pl.BlockSpec((tm, d), lambda i: (i, 0)),
        out_shape=jax.ShapeDtypeStruct((t, d), F32),
        scratch_shapes=[pltpu.VMEM((2, TOP_K, tm, d), F32), pltpu.SemaphoreType.DMA((2,))],
        compiler_params=_cp("arbitrary"),
        name="moe_combine",
    )(dest3, dest3, out_rows, gates, x1, gate2, normf_w.reshape(1, d))


def _moe_plan(idx, rank, counts, n_experts):
    bm = MOE_ROWS
    n_assign = idx.shape[0] * TOP_K
    nblk = (counts + bm - 1) // bm
    blk_end = jnp.cumsum(nblk)
    blk_start = blk_end - nblk
    dest = (bm * blk_start[idx] + rank).astype(jnp.int32)
    nb = (n_assign + n_experts * (bm - 1) + bm - 1) // bm
    n_active = blk_end[-1].astype(jnp.int32)
    gaps = jnp.stack([bm * blk_start + counts, bm * nblk - counts, jnp.full_like(counts, n_active)]).astype(jnp.int32)
    return dest, gaps, blk_start.astype(jnp.int32), nblk.astype(jnp.int32), n_active, nb


def kernel(x, c, positions, w_ada, b_ada, norm1_w, w_in, rwkv_mu_rkvg, rwkv_mu_wa, rwkv_w0, rwkv_w1, rwkv_w2, rwkv_a0, rwkv_a1, rwkv_a2, rwkv_k_k, rwkv_k_a, rwkv_r_k, rwkv_lnx_w, rwkv_lnx_b, w_out, norm2_w, w_router, b_router, w_gate_up, b_gate_up, w_down, b_down, normf_w):
    bsz, seq, d = x.shape
    t = bsz * seq
    depth = w_ada.shape[0]
    rw = rwkv_w0.shape[1]
    ret_w = d - rw
    n_experts = w_router.shape[2]
    x2d = x.reshape(t, d)
    for l in range(depth):
        mod = _adaln(c, w_ada[l], b_ada[l])
        shift1, scale1, gate1, shift2, scale2, gate2 = [m.reshape(bsz, 1, d) for m in jnp.split(mod, 6, axis=-1)]
        h, tw, la = _norm1_lora(x2d, norm1_w[l].reshape(1, d), scale1, shift1,
                                rwkv_mu_wa[l, 0].reshape(1, d), rwkv_mu_wa[l, 1].reshape(1, d),
                                rwkv_w1[l].astype(BF16), rwkv_a1[l].astype(BF16), seq)
        p = _matmul(h, w_in[l].astype(BF16))
        mu = rwkv_mu_rkvg[l].reshape(4, 1, rw)
        prm = (mu[0], mu[1], mu[2], mu[3], rwkv_w0[l].reshape(1, rw), rwkv_a0[l].reshape(1, rw),
               rwkv_k_k[l].reshape(1, rw), rwkv_k_a[l].reshape(1, rw), rwkv_r_k[l].reshape(1, rw),
               rwkv_lnx_w[l].reshape(1, rw), rwkv_lnx_b[l].reshape(1, rw))
        y_rwkv = _rwkv(p, tw, la, prm, rwkv_w2[l].astype(BF16), rwkv_a2[l].astype(BF16), bsz, seq, rw)
        y_ret = _retention(p, positions, bsz, seq, 4 * rw, ret_w)
        wo = w_out[l].astype(BF16)
        x2d = _outproj(y_rwkv, y_ret, wo[:rw], wo[rw:], x2d, gate1, seq)
        h2, idx, gates, rank, counts = _router(x2d, norm2_w[l].reshape(1, d), scale2, shift2,
                                               w_router[l], b_router[l], seq)
        counts = counts[0, :n_experts]
        dest, gaps, blk_start, nblk, n_active, nb = _moe_plan(idx[:, :TOP_K], rank[:, :TOP_K], counts, n_experts)
        rows = _dispatch(h2, dest, gaps, nb * MOE_ROWS)
        de = w_down.shape[2]
        up_tiles = de // min(MOE_UP_TILE, de)
        act = _gate_up(rows, _moe_steps(counts, blk_start, nblk, n_active, up_tiles, nb), w_gate_up[l], b_gate_up[l])
        tn_down = min(MOE_DOWN_TILE, d)
        out_rows = _down(act, _moe_steps(counts, blk_start, nblk, n_active, d // tn_down, nb), w_down[l], b_down[l], tn_down)
        if l + 1 < depth:
            raise NotImplementedError("the final norm is fused into the last layer's combine")
        out = _combine(out_rows, dest, gates, x2d, gate2, normf_w, seq)
    return out.reshape(bsz, seq, d)
```

```python
import functools

import jax
import jax.numpy as jnp
from jax import lax
from jax.experimental import pallas as pl
from jax.experimental.pallas import tpu as pltpu

F32 = jnp.float32
BF16 = jnp.bfloat16

RWKV_HEAD = 64
RET_HEAD = 256
RET_CHUNK = 128
ROPE_BASE = 10000.0
TOP_K = 4
SWIGLU_LIMIT = 7.0
SWIGLU_ALPHA = 1.702
NORM_EPS = 1e-5
RWKV_GN_EPS = 64e-5
RET_GN_EPS = 1e-6

LANES = 128
SUBLANES = 8
RWKV_CHUNK = 64
RWKV_GROUP = 256
HEADS_PER_GROUP = RWKV_GROUP // RWKV_HEAD
MOE_ROWS = 512
MOE_UP_TILE = 256
MOE_DOWN_TILE = 2048
VMEM_LIMIT = 56 << 20


def _cp(*sem):
    return pltpu.CompilerParams(dimension_semantics=sem, vmem_limit_bytes=VMEM_LIMIT)


def _dot(a, b):
    return jnp.dot(a, b, preferred_element_type=F32)


def _dot_nt(a, b):
    return lax.dot_general(a, b, (((1,), (1,)), ((), ())), preferred_element_type=F32)


def _dot_tn(a, b):
    return lax.dot_general(a, b, (((0,), (0,)), ((), ())), preferred_element_type=F32)


def _split2(x):
    hi = x.astype(BF16)
    lo = (x - hi.astype(F32)).astype(BF16)
    return hi, lo


def _rms(x, w):
    return x * lax.rsqrt(jnp.mean(x * x, axis=-1, keepdims=True) + NORM_EPS) * w


def _adaln_body(c_ref, w_ref, b_ref, o_ref):
    c = c_ref[...]
    ca = c * jax.nn.sigmoid(c)
    o_ref[...] = _dot(ca.astype(BF16), w_ref[...].astype(BF16)) + b_ref[...]


def _adaln(c, w_ada, b_ada):
    bsz, d = c.shape
    n = w_ada.shape[1]
    tn = min(512, n)
    c8 = jnp.zeros((8, d), F32).at[:bsz].set(c)
    out = pl.pallas_call(
        _adaln_body,
        grid=(n // tn,),
        in_specs=[pl.BlockSpec((8, d), lambda j: (0, 0)),
                  pl.BlockSpec((d, tn), lambda j: (0, j)),
                  pl.BlockSpec((1, tn), lambda j: (0, j))],
        out_specs=pl.BlockSpec((8, tn), lambda j: (0, j)),
        out_shape=jax.ShapeDtypeStruct((8, n), F32),
        compiler_params=_cp("arbitrary"),
        name="adaln",
    )(c8, w_ada, b_ada.reshape(1, n))
    return out[:bsz]


def _norm1_body(tiles_per_seq, x_ref, nw_ref, sc_ref, sh_ref, muw_ref, mua_ref, w1_ref, a1_ref,
                h_ref, tw_ref, la_ref, carry):
    i = pl.program_id(0)

    @pl.when(i % tiles_per_seq == 0)
    def _():
        carry[...] = jnp.zeros_like(carry)

    x = x_ref[...]
    tm = x.shape[0]
    h = _rms(x, nw_ref[...]) * (1.0 + sc_ref[...]) + sh_ref[...]
    row = lax.broadcasted_iota(jnp.int32, h.shape, 0)
    h_prev = jnp.where(row == 0, carry[0:1, :], pltpu.roll(h, 1, 0))
    carry[0:1, :] = h[tm - 1:tm, :]
    dh = h_prev - h
    xw = h + dh * muw_ref[...]
    xa = h + dh * mua_ref[...]
    tw_ref[...] = jnp.tanh(_dot(xw.astype(BF16), w1_ref[...]))
    la_ref[...] = _dot(xa.astype(BF16), a1_ref[...])
    h_ref[...] = h.astype(BF16)


def _norm1_lora(x2d, nw, scale, shift, mu_w, mu_a, w1b, a1b, seq):
    t, d = x2d.shape
    tm = min(256, seq)
    tps = seq // tm
    r = w1b.shape[1]
    row_d = pl.BlockSpec((1, d), lambda i: (0, 0))
    mod = pl.BlockSpec((None, 1, d), lambda i: (i // tps, 0, 0))
    lora = pl.BlockSpec((d, r), lambda i: (0, 0))
    return pl.pallas_call(
        functools.partial(_norm1_body, tps),
        grid=(t // tm,),
        in_specs=[pl.BlockSpec((tm, d), lambda i: (i, 0)), row_d, mod, mod, row_d, row_d, lora, lora],
        out_specs=[pl.BlockSpec((tm, d), lambda i: (i, 0)),
                   pl.BlockSpec((tm, r), lambda i: (i, 0)),
                   pl.BlockSpec((tm, r), lambda i: (i, 0))],
        out_shape=[jax.ShapeDtypeStruct((t, d), BF16),
                   jax.ShapeDtypeStruct((t, r), F32),
                   jax.ShapeDtypeStruct((t, r), F32)],
        scratch_shapes=[pltpu.VMEM((8, d), F32)],
        compiler_params=_cp("arbitrary"),
        name="norm1_lora",
    )(x2d, nw, scale, shift, mu_w, mu_a, w1b, a1b)


def _matmul_body(a_ref, b_ref, o_ref):
    o_ref[...] = _dot(a_ref[...], b_ref[...])


def _matmul(a, b):
    m, k = a.shape
    n = b.shape[1]
    tm, tn = min(1024, m), min(512, n)
    return pl.pallas_call(
        _matmul_body,
        grid=(m // tm, n // tn),
        in_specs=[pl.BlockSpec((tm, k), lambda i, j: (i, 0)),
                  pl.BlockSpec((k, tn), lambda i, j: (0, j))],
        out_specs=pl.BlockSpec((tm, tn), lambda i, j: (i, j)),
        out_shape=jax.ShapeDtypeStruct((m, n), F32),
        compiler_params=_cp("parallel", "arbitrary"),
        name="in_proj",
    )(a, b)


def _rwkv_body(tb, pr_ref, pk_ref, pv_ref, pg_ref, tw_ref, la_ref,
               mur_ref, muk_ref, muv_ref, mug_ref, w0_ref, a0_ref, kk_ref, ka_ref, rk_ref,
               lnw_ref, lnb_ref, w2_ref, a2_ref,
               bd_ref, mstk_ref, sm_ref, im_ref, tri_ref, eye_ref, lvl_ref,
               y_ref,
               carry, state):
    i = pl.program_id(2)
    c = RWKV_CHUNK

    @pl.when(i == 0)
    def _():
        carry[...] = jnp.zeros_like(carry)
        state[...] = jnp.zeros_like(state)

    row = lax.broadcasted_iota(jnp.int32, (tb, RWKV_GROUP), 0)

    def shift_mix(ref, slot, mu):
        cur = ref[...]
        prev = jnp.where(row == 0, carry[slot:slot + 1, :], pltpu.roll(cur, 1, 0))
        carry[slot:slot + 1, :] = cur[tb - 1:tb, :]
        return cur + (prev - cur) * mu

    bd = bd_ref[...]

    def seg_sum(x):
        hi, lo = _split2(x)
        return _dot(hi, bd) + _dot(lo, bd)

    r = shift_mix(pr_ref, 0, mur_ref[...])
    k = shift_mix(pk_ref, 1, muk_ref[...])
    v = shift_mix(pv_ref, 2, muv_ref[...])
    g = shift_mix(pg_ref, 3, mug_ref[...])

    z = -(w0_ref[...] + _dot(tw_ref[...].astype(BF16), w2_ref[...]))
    softplus = jnp.maximum(z, 0.0) + jnp.log(1.0 + jnp.exp(-jnp.abs(z)))
    lw = -jnp.exp(-softplus - 0.5)
    a_icl = jax.nn.sigmoid(a0_ref[...] + _dot(la_ref[...].astype(BF16), a2_ref[...]))
    kk = k * kk_ref[...]
    kk = kk * lax.rsqrt(jnp.maximum(seg_sum(kk * kk), 1e-24))
    k = k * (1.0 + (a_icl - 1.0) * ka_ref[...])
    a_in = -kk
    b_in = kk * a_icl

    mstk = mstk_ref[...]
    strict = sm_ref[...] > 0.0
    incl = im_ref[...] > 0.0
    same_head = bd > 0
    tri = tri_ref[...]

    def stack(x):
        return jnp.concatenate([x.astype(BF16)] * HEADS_PER_GROUP, axis=0) * mstk

    n_ch = tb // c

    def each(fn, *cols):
        return [fn(*args) for args in zip(*cols)]

    rows = [slice(ci * c, (ci + 1) * c) for ci in range(n_ch)]
    lw_c = [lw[sl] for sl in rows]

    def cumsum(x):
        hi, lo = _split2(x)
        return _dot(tri, hi) + _dot(tri, lo)

    cum = each(cumsum, lw_c)
    cum_end = [x[c - 1:c, :] for x in cum]
    p_inv = each(lambda x: jnp.exp(-x), cum)
    p_end = each(lambda x, e: jnp.exp(e - x), cum, cum_end)
    a_d = each(lambda sl, x, l: (a_in[sl] * jnp.exp(x - l)).astype(BF16), rows, cum, lw_c)
    r_d = each(lambda sl, x: r[sl] * jnp.exp(x), rows, cum)
    ar_b = each(lambda x, y: jnp.concatenate([x, y.astype(BF16)], axis=0), a_d, r_d)
    a_t = each(stack, a_d)
    b_t = each(lambda sl, p: stack(b_in[sl] * p), rows, p_inv)
    k_t = each(lambda sl, p: stack(k[sl] * p), rows, p_inv)
    v_t = each(lambda sl: stack(v[sl]), rows)
    b_h = each(lambda sl, p: (b_in[sl] * p).astype(BF16), rows, p_end)
    uk_r = each(lambda sl, p: (k[sl] * p).astype(BF16), rows, p_end)
    by_b = each(_dot_nt, ar_b, b_t)
    by_k = each(_dot_nt, ar_b, k_t)
    a_ab = each(lambda x: jnp.where(strict, x[0:c], 0.0), by_b)
    a_ak = each(lambda x: jnp.where(strict, x[0:c], 0.0).astype(BF16), by_k)
    a_rb = each(lambda x: jnp.where(incl, x[c:], 0.0).astype(BF16), by_b)
    a_rk = each(lambda x: jnp.where(incl, x[c:], 0.0).astype(BF16), by_k)
    inv = each(lambda x: eye_ref[...] + x * lvl_ref[0], a_ab)
    for lv in range(1, c.bit_length() - 1):
        inv_bd = each(stack, inv)
        w = each(lambda x, t: _dot((x * lvl_ref[lv]).astype(BF16), t), a_ab, inv_bd)
        inv = each(lambda x, ww: x + _dot(x.astype(BF16), stack(ww)), inv, w)
    invb = each(lambda x: x.astype(BF16), inv)
    a_p = each(_dot, invb, a_t)
    akv = each(_dot, a_ak, v_t)
    u_loc = each(lambda t, x: _dot(t, stack(x)), invb, akv)
    r_p = each(lambda x, m, y: (x + _dot(m, stack(y))).astype(BF16), r_d, a_rb, a_p)
    y_loc = each(lambda m, x, n, y: _dot(m, stack(x)) + _dot(n, y), a_rb, u_loc, a_rk, v_t)
    phi = each(lambda x, y: jnp.where(same_head, _dot_tn(x.astype(BF16), y), 0.0).astype(BF16), a_p, b_h)
    s_loc = each(lambda x, y, sl, z: jnp.where(
        same_head,
        _dot_tn(jnp.concatenate([x.astype(BF16), v[sl].astype(BF16)], axis=0), jnp.concatenate([y, z], axis=0)),
        0.0), u_loc, b_h, rows, uk_r)

    s = state[...]
    ys = []
    for ci in range(n_ch):
        sb = s.astype(BF16)
        ys.append(_dot_nt(r_p[ci], sb) + y_loc[ci])
        s = s * jnp.exp(cum_end[ci]) + _dot(sb, phi[ci]) + s_loc[ci]
    state[...] = s

    y = jnp.concatenate(ys, axis=0)
    inv_n = 1.0 / RWKV_HEAD
    mean = seg_sum(y) * inv_n
    dlt = y - mean
    var = seg_sum(dlt * dlt) * inv_n
    yn = dlt * lax.rsqrt(var + RWKV_GN_EPS) * lnw_ref[...] + lnb_ref[...]
    bonus = seg_sum(r * k * rk_ref[...]) * v
    y_ref[...] = ((yn + bonus) * jax.nn.sigmoid(g)).astype(BF16)


def _rwkv_consts():
    n = RWKV_GROUP
    c = RWKV_CHUNK
    lane = jnp.arange(n)
    head = lane // RWKV_HEAD
    blk = lane // c
    bd = (head[:, None] == head[None, :]).astype(BF16)
    mstk = (blk[:, None] == head[None, :]).astype(BF16)
    t_idx = jnp.arange(c)[:, None]
    s_idx = (lane % c)[None, :]
    strict = (s_idx < t_idx).astype(F32)
    incl = (s_idx <= t_idx).astype(F32)
    eye = (s_idx == t_idx).astype(F32)
    tri = (jnp.arange(c)[None, :] <= jnp.arange(c)[:, None]).astype(BF16)
    lvls = []
    for lv in range(c.bit_length() - 1):
        m = 1 << lv
        hi_lo = ((t_idx // m) % 2 == 1) & ((s_idx // m) % 2 == 0)
        lvls.append((hi_lo & (t_idx // (2 * m) == s_idx // (2 * m))).astype(F32))
    return bd, mstk, strict, incl, tri, eye, jnp.stack(lvls)


def _rwkv(p, tw, la, prm, w2b, a2b, bsz, seq, rw):
    t = p.shape[0]
    g = rw // RWKV_GROUP
    tb = min(1024, seq)
    nt = seq // tb
    r = tw.shape[1]
    bd, mstk, strict, incl, tri, eye, lvls = _rwkv_consts()

    def pcol(m):
        return pl.BlockSpec((tb, RWKV_GROUP), lambda b, gg, i: (b * nt + i, m * g + gg))

    lor = pl.BlockSpec((tb, r), lambda b, gg, i: (b * nt + i, 0))
    chan = pl.BlockSpec((1, RWKV_GROUP), lambda b, gg, i: (0, gg))
    lw2 = pl.BlockSpec((r, RWKV_GROUP), lambda b, gg, i: (0, gg))
    sq = pl.BlockSpec((RWKV_GROUP, RWKV_GROUP), lambda b, gg, i: (0, 0))
    tri_spec = pl.BlockSpec((RWKV_CHUNK, RWKV_CHUNK), lambda b, gg, i: (0, 0))
    wide = pl.BlockSpec((RWKV_CHUNK, RWKV_GROUP), lambda b, gg, i: (0, 0))
    return pl.pallas_call(
        functools.partial(_rwkv_body, tb),
        grid=(bsz, g, nt),
        in_specs=[pcol(0), pcol(1), pcol(2), pcol(3), lor, lor] + [chan] * 11 + [lw2, lw2]
                 + [sq, sq, wide, wide, tri_spec, wide, pl.BlockSpec(lvls.shape, lambda b, gg, i: (0, 0, 0))],
        out_specs=pl.BlockSpec((tb, RWKV_GROUP), lambda b, gg, i: (b * nt + i, gg)),
        out_shape=jax.ShapeDtypeStruct((t, rw), BF16),
        scratch_shapes=[pltpu.VMEM((8, RWKV_GROUP), F32), pltpu.VMEM((RWKV_GROUP, RWKV_GROUP), F32)],
        compiler_params=_cp("parallel", "parallel", "arbitrary"),
        name="rwkv7",
    )(p, p, p, p, tw, la, *prm, w2b, a2b, bd, mstk, strict, incl, tri, eye, lvls)


def _ret_body(heads, q_ref, k_ref, v_ref, g_ref, pos_ref, invf_ref, inner_ref, qd_ref, kd_ref, cd_ref,
              o_ref, state):
    @pl.when(pl.program_id(1) == 0)
    def _():
        state[...] = jnp.zeros_like(state)

    half = RET_HEAD // 2
    ang = pos_ref[...].astype(F32) * invf_ref[...]
    cos = jnp.cos(ang)
    sin = jnp.sin(ang)

    def rope(x):
        x1, x2 = x[:, :half], x[:, half:]
        return jnp.concatenate([x1 * cos - x2 * sin, x1 * sin + x2 * cos], axis=-1)

    for hh in range(heads):
        sl = slice(hh * RET_HEAD, (hh + 1) * RET_HEAD)
        q = rope(q_ref[:, sl])
        k = rope(k_ref[:, sl]) * (RET_HEAD ** -0.5)
        v = v_ref[:, sl].astype(BF16)
        g = g_ref[:, sl]
        qb = q.astype(BF16)
        scores = _dot_nt(qb, k.astype(BF16)) * inner_ref[hh]
        st = state[hh]
        o = _dot(scores.astype(BF16), v) + _dot(qb, st.astype(BF16)) * qd_ref[hh]
        state[hh] = st * cd_ref[hh] + _dot_tn((k * kd_ref[hh]).astype(BF16), v)
        on = o * lax.rsqrt(jnp.mean(o * o, axis=-1, keepdims=True) + RET_GN_EPS)
        o_ref[:, sl] = (on * (g * jax.nn.sigmoid(g))).astype(BF16)


def _retention(p, positions, bsz, seq, col0, width):
    t = p.shape[0]
    heads = width // RET_HEAD
    c = RET_CHUNK
    nc = seq // c
    half = RET_HEAD // 2
    cb = col0 // width
    inv_freq = (ROPE_BASE ** (-jnp.arange(half, dtype=F32) / half)).reshape(1, half)
    log_gamma = jnp.log(1.0 - 2.0 ** (-5.0 - jnp.arange(heads, dtype=F32)))
    idx = jnp.arange(c, dtype=F32)
    diff = idx[:, None] - idx[None, :]
    inner = jnp.where(diff >= 0, jnp.exp(log_gamma[:, None, None] * jnp.maximum(diff, 0.0)), 0.0)
    q_decay = jnp.exp(log_gamma[:, None] * (idx + 1.0))[:, :, None]
    k_decay = jnp.exp(log_gamma[:, None] * (c - 1.0 - idx))[:, :, None]
    chunk_decay = jnp.broadcast_to(jnp.exp(log_gamma * c)[:, None, None], (heads, 1, RET_HEAD))

    def pcol(m):
        return pl.BlockSpec((c, width), lambda b, i: (b * nc + i, cb + m))

    def full(a):
        return pl.BlockSpec(a.shape, lambda b, i: (0,) * a.ndim)

    consts = (inv_freq, inner, q_decay, k_decay, chunk_decay)
    return pl.pallas_call(
        functools.partial(_ret_body, heads),
        grid=(bsz, nc),
        in_specs=[pcol(0), pcol(1), pcol(2), pcol(3),
                  pl.BlockSpec((c, 1), lambda b, i: (b * nc + i, 0))] + [full(a) for a in consts],
        out_specs=pl.BlockSpec((c, width), lambda b, i: (b * nc + i, 0)),
        out_shape=jax.ShapeDtypeStruct((t, width), BF16),
        scratch_shapes=[pltpu.VMEM((heads, RET_HEAD, RET_HEAD), F32)],
        compiler_params=_cp("parallel", "arbitrary"),
        name="retention",
    )(p, p, p, p, positions.reshape(t, 1), *consts)


def _outproj_body(ya_ref, yb_ref, wa_ref, wb_ref, x_ref, g_ref, o_ref):
    mix = _dot(ya_ref[...], wa_ref[...]) + _dot(yb_ref[...], wb_ref[...])
    o_ref[...] = x_ref[...] + g_ref[...] * mix


def _outproj(ya, yb, wa, wb, x2d, gate, seq):
    t, d = x2d.shape
    ka, kb = ya.shape[1], yb.shape[1]
    tm, tn = min(1024, seq), min(512, d)
    tps = seq // tm
    return pl.pallas_call(
        _outproj_body,
        grid=(t // tm, d // tn),
        in_specs=[pl.BlockSpec((tm, ka), lambda i, j: (i, 0)),
                  pl.BlockSpec((tm, kb), lambda i, j: (i, 0)),
                  pl.BlockSpec((ka, tn), lambda i, j: (0, j)),
                  pl.BlockSpec((kb, tn), lambda i, j: (0, j)),
                  pl.BlockSpec((tm, tn), lambda i, j: (i, j)),
                  pl.BlockSpec((None, 1, tn), lambda i, j: (i // tps, 0, j))],
        out_specs=pl.BlockSpec((tm, tn), lambda i, j: (i, j)),
        out_shape=jax.ShapeDtypeStruct((t, d), F32),
        compiler_params=_cp("parallel", "arbitrary"),
        name="out_proj",
    )(ya, yb, wa, wb, x2d, gate)


def _pack_bf16_pairs(h):
    half = h.shape[1] // 2
    hi = lax.bitcast_convert_type(h[:, :half].astype(BF16).astype(F32), jnp.int32)
    lo = lax.bitcast_convert_type(h[:, half:].astype(BF16).astype(F32), jnp.int32)
    return hi | lax.shift_right_logical(lo, 16)


def _unpack_bf16_pairs(w):
    hi = lax.bitcast_convert_type(w & -65536, F32).astype(BF16)
    lo = lax.bitcast_convert_type(w << 16, F32).astype(BF16)
    return hi, lo


def _router_body(x_ref, nw_ref, sc_ref, sh_ref, wrh_ref, wrl_ref, br_ref, tri_ref,
                 h_ref, idx_ref, gate_ref, rank_ref, cnt_ref, carry):
    @pl.when(pl.program_id(0) == 0)
    def _():
        carry[...] = jnp.zeros_like(carry)

    h = _rms(x_ref[...], nw_ref[...]) * (1.0 + sc_ref[...]) + sh_ref[...]
    h_ref[...] = _pack_bf16_pairs(h)
    h_hi, h_lo = _split2(h)
    logits = _dot(h_hi, wrh_ref[...]) + _dot(h_hi, wrl_ref[...]) + _dot(h_lo, wrh_ref[...]) + br_ref[...]
    lane = lax.broadcasted_iota(jnp.int32, logits.shape, 1)
    vals, idxs = [], []
    rest = logits
    for _ in range(TOP_K):
        m = jnp.max(rest, axis=-1, keepdims=True)
        ix = jnp.min(jnp.where(rest == m, lane, LANES), axis=-1, keepdims=True)
        vals.append(m)
        idxs.append(ix)
        rest = jnp.where(lane == ix, -jnp.inf, rest)
    exps = [jnp.exp(vv - vals[0]) for vv in vals]
    den = exps[0]
    for e in exps[1:]:
        den = den + e
    hot = (lane == idxs[0]).astype(F32)
    for ix in idxs[1:]:
        hot = hot + (lane == ix).astype(F32)
    before = _dot(tri_ref[...], hot.astype(BF16)) + carry[0:1, :]
    carry[0:1, :] = carry[0:1, :] + jnp.sum(hot, axis=0, keepdims=True)
    idx_out = jnp.zeros(logits.shape, jnp.int32)
    gate_out = jnp.zeros(logits.shape, F32)
    rank_out = jnp.zeros(logits.shape, F32)
    for kk in range(TOP_K):
        rk = jnp.sum(jnp.where(lane == idxs[kk], before, 0.0), axis=-1, keepdims=True)
        idx_out = jnp.where(lane == kk, idxs[kk], idx_out)
        gate_out = jnp.where(lane == kk, exps[kk] / den, gate_out)
        rank_out = jnp.where(lane == kk, rk, rank_out)
    idx_ref[...] = idx_out
    gate_ref[...] = gate_out
    rank_ref[...] = rank_out.astype(jnp.int32)
    cnt_ref[...] = jnp.broadcast_to(carry[0:1, :], cnt_ref.shape).astype(jnp.int32)


def _router(x2d, nw, scale, shift, w_router, b_router, seq):
    t, d = x2d.shape
    e = w_router.shape[1]
    tm = min(256, seq)
    tps = seq // tm
    wr = jnp.zeros((d, LANES), F32).at[:, :e].set(w_router)
    wr_hi, wr_lo = _split2(wr)
    br = jnp.full((1, LANES), -1e30, F32).at[0, :e].set(b_router)
    tri = (jnp.arange(tm)[None, :] < jnp.arange(tm)[:, None]).astype(BF16)
    row_d = pl.BlockSpec((1, d), lambda i: (0, 0))
    mod = pl.BlockSpec((None, 1, d), lambda i: (i // tps, 0, 0))
    lane_out = pl.BlockSpec((tm, LANES), lambda i: (i, 0))
    return pl.pallas_call(
        _router_body,
        grid=(t // tm,),
        in_specs=[pl.BlockSpec((tm, d), lambda i: (i, 0)), row_d, mod, mod,
                  pl.BlockSpec((d, LANES), lambda i: (0, 0)),
                  pl.BlockSpec((d, LANES), lambda i: (0, 0)),
                  pl.BlockSpec((1, LANES), lambda i: (0, 0)),
                  pl.BlockSpec((tm, tm), lambda i: (0, 0))],
        out_specs=[pl.BlockSpec((tm, d // 2), lambda i: (i, 0)), lane_out, lane_out, lane_out,
                   pl.BlockSpec((8, LANES), lambda i: (0, 0))],
        out_shape=[jax.ShapeDtypeStruct((t, d // 2), jnp.int32),
                   jax.ShapeDtypeStruct((t, LANES), jnp.int32),
                   jax.ShapeDtypeStruct((t, LANES), F32),
                   jax.ShapeDtypeStruct((t, LANES), jnp.int32),
                   jax.ShapeDtypeStruct((8, LANES), jnp.int32)],
        scratch_shapes=[pltpu.VMEM((8, LANES), F32)],
        compiler_params=_cp("arbitrary"),
        name="norm2_router",
    )(x2d, nw, scale, shift, wr_hi, wr_lo, br, tri)


def _row_copy(src, dst, sem):
    return pltpu.make_async_copy(src, dst, sem)


def _dispatch_body(tm, gaps_ref, dest_ref, h_ref, rows_out, zeros, sem, zsem):
    zr, half = zeros.shape
    n_experts = gaps_ref.shape[1]

    def zero_fill(start_or_wait):
        def gap(e, carry_):
            pos = gaps_ref[0, e]
            length = gaps_ref[1, e]
            head = jnp.minimum((-pos) % SUBLANES, length)
            n_tiles = (length - head) // SUBLANES
            body = pos + head
            tail = body + n_tiles * SUBLANES

            def row(at):
                def one(r, c2):
                    start_or_wait(pltpu.make_async_copy(zeros.at[pl.ds(0, 1)], rows_out.at[pl.ds(at + r, 1)], zsem))
                    return c2
                return one

            def tile(r, c2):
                dst = rows_out.at[pl.ds(pl.multiple_of(body + r * SUBLANES, SUBLANES), SUBLANES)]
                start_or_wait(pltpu.make_async_copy(zeros.at[pl.ds(0, SUBLANES)], dst, zsem))
                return c2

            lax.fori_loop(0, head, row(pos), 0)
            lax.fori_loop(0, n_tiles, tile, 0)
            lax.fori_loop(0, length - head - n_tiles * SUBLANES, row(tail), 0)
            return carry_

        lax.fori_loop(0, n_experts, gap, 0)

        def spare(b, carry_):
            for hh in range(MOE_ROWS // zr):
                dst = rows_out.at[pl.ds(pl.multiple_of(b * MOE_ROWS + hh * zr, zr), zr)]
                start_or_wait(pltpu.make_async_copy(zeros, dst, zsem))
            return carry_

        lax.fori_loop(gaps_ref[2, 0], rows_out.shape[0] // MOE_ROWS, spare, 0)

    @pl.when(pl.program_id(0) == 0)
    def _():
        zeros[...] = jnp.zeros_like(zeros)
        zero_fill(lambda cp: cp.start())

    def issue(r, carry_):
        for kk in range(TOP_K):
            d = dest_ref[0, r * TOP_K + kk]
            _row_copy(h_ref.at[pl.ds(r, 1)], rows_out.at[pl.ds(d, 1)], sem).start(priority=kk % 2)
        return carry_

    def drain(r, carry_):
        for _ in range(TOP_K):
            _row_copy(h_ref.at[pl.ds(0, 1)], rows_out.at[pl.ds(0, 1)], sem).wait()
        return carry_

    lax.fori_loop(0, tm, issue, 0)
    lax.fori_loop(0, tm, drain, 0)

    @pl.when(pl.program_id(0) == 0)
    def _():
        zero_fill(lambda cp: cp.wait())


def _dispatch(h2, dest, gaps, cap):
    t, d = h2.shape
    tm = min(128, t)
    dest3 = dest.reshape(t // tm, 1, tm * TOP_K)
    return pl.pallas_call(
        functools.partial(_dispatch_body, tm),
        grid=(t // tm,),
        in_specs=[pl.BlockSpec(memory_space=pltpu.SMEM),
                  pl.BlockSpec((None, 1, tm * TOP_K), lambda i: (i, 0, 0), memory_space=pltpu.SMEM),
                  pl.BlockSpec((tm, d), lambda i: (i, 0))],
        out_specs=pl.BlockSpec(memory_space=pl.ANY),
        out_shape=jax.ShapeDtypeStruct((cap, d), h2.dtype),
        scratch_shapes=[pltpu.VMEM((MOE_ROWS // 2, d), h2.dtype), pltpu.SemaphoreType.DMA, pltpu.SemaphoreType.DMA],
        compiler_params=_cp("arbitrary"),
        name="moe_dispatch",
    )(gaps, dest3, h2)


def _moe_steps(counts, blk_start, nblk, n_active, n_tiles, nb):
    n_experts = blk_start.shape[0]
    total = n_tiles * nb
    s = jnp.arange(total, dtype=jnp.int32)
    step_end = n_tiles * (blk_start + nblk)
    e = jnp.minimum(jnp.sum(step_end[None, :] <= s[:, None], axis=1), n_experts - 1).astype(jnp.int32)
    local = s - n_tiles * blk_start[e]
    per = jnp.maximum(nblk[e], 1)
    tile = local // per
    rb = blk_start[e] + local % per
    first = local % per == 0
    n_steps = (n_tiles * n_active).astype(jnp.int32)
    active = s < n_steps
    last = n_steps - 1
    spare = s - n_steps
    slot = (jnp.cumsum(first.astype(jnp.int32)) - 1) % 2
    nxt = jnp.minimum(s - local % per + per, total - 1)
    cols = (jnp.where(active, e, e[last]), jnp.where(active, tile, tile[last]), jnp.where(active, rb, rb[last]),
            jnp.where(active, tile, spare % n_tiles), jnp.where(active, rb, n_active + spare // n_tiles),
            first, slot, e[nxt], tile[nxt], nxt < n_steps,
            counts[e] - MOE_ROWS * (local % per) <= MOE_ROWS // 2)
    return tuple(x.astype(jnp.int32) for x in cols) + (n_steps.reshape(1),)


N_STEP_ARRAYS = 12


def _weight_stream(s, first, slot, e_cur, t_cur, e_nxt, t_nxt, has_nxt, copies, convert):
    @pl.when(first[s] == 1)
    def _():
        sl = slot[s]

        @pl.when(s == 0)
        def _():
            for cp in copies(e_cur[0], t_cur[0], 0):
                cp.start()

        for cp in copies(e_cur[s], t_cur[s], sl):
            cp.wait()
        convert(sl)

        @pl.when(has_nxt[s] == 1)
        def _():
            for cp in copies(e_nxt[s], t_nxt[s], 1 - sl):
                cp.start()


def _gate_up_body(se, st, sri, sto, sro, first, slot, en, tn_, hn, half_blk, n_steps, x_ref, w_hbm, bg_ref, bu_ref,
                  act_ref, wf, wgb, wub, sem):
    s = pl.program_id(0)
    d, tn = wgb.shape
    nj = w_hbm.shape[2] // (2 * tn)
    bm = x_ref.shape[0]

    @pl.when(s >= n_steps[0])
    def _():
        act_ref[...] = jnp.zeros_like(act_ref)

    @pl.when(s < n_steps[0])
    def _():
        def copies(e, j, sl):
            return [pltpu.make_async_copy(w_hbm.at[e, :, pl.ds(pl.multiple_of((m * nj + j) * tn, tn), tn)],
                                          wf.at[sl, m], sem.at[sl]) for m in range(2)]

        def convert(sl):
            wgb[...] = wf[sl, 0].astype(BF16)
            wub[...] = wf[sl, 1].astype(BF16)

        _weight_stream(s, first, slot, se, st, en, tn_, hn, copies, convert)
        half = d // 2

        def rows(n):
            xa, xb = _unpack_bf16_pairs(x_ref[0:n, :])

            def proj(w, b_ref):
                return _dot(xa, w[:half, :]) + _dot(xb, w[half:, :]) + b_ref[...]

            gate = jnp.minimum(proj(wgb, bg_ref), SWIGLU_LIMIT)
            up = jnp.clip(proj(wub, bu_ref), -SWIGLU_LIMIT, SWIGLU_LIMIT)
            act_ref[0:n, :] = ((up + 1.0) * gate * jax.nn.sigmoid(SWIGLU_ALPHA * gate)).astype(BF16)

        @pl.when(half_blk[s] == 0)
        def _():
            rows(bm)

        @pl.when(half_blk[s] == 1)
        def _():
            rows(bm // 2)
            act_ref[bm // 2:, :] = jnp.zeros((bm - bm // 2, tn), BF16)


def _gate_up(rows, steps, wgu, bgu):
    cap, half = rows.shape
    d = 2 * half
    e, _, two_de = wgu.shape
    de = two_de // 2
    bm = MOE_ROWS
    tn = min(MOE_UP_TILE, de)
    nj = de // tn
    grid_spec = pltpu.PrefetchScalarGridSpec(
        num_scalar_prefetch=N_STEP_ARRAYS,
        grid=(nj * (cap // bm),),
        in_specs=[pl.BlockSpec((bm, half), lambda s, se, st, sri, *_: (sri[s], 0)),
                  pl.BlockSpec(memory_space=pl.ANY),
                  pl.BlockSpec((None, 1, tn), lambda s, se, st, *_: (se[s], 0, st[s])),
                  pl.BlockSpec((None, 1, tn), lambda s, se, st, *_: (se[s], 0, nj + st[s]))],
        out_specs=pl.BlockSpec((bm, tn), lambda s, se, st, sri, sto, sro, *_: (sro[s], sto[s])),
        scratch_shapes=[pltpu.VMEM((2, 2, d, tn), F32), pltpu.VMEM((d, tn), BF16), pltpu.VMEM((d, tn), BF16),
                        pltpu.SemaphoreType.DMA((2,))],
    )
    bgu3 = bgu.reshape(e, 1, two_de)
    return pl.pallas_call(
        _gate_up_body,
        grid_spec=grid_spec,
        out_shape=jax.ShapeDtypeStruct((cap, de), BF16),
        compiler_params=_cp("arbitrary"),
        name="moe_gate_up",
    )(*steps, rows, wgu, bgu3, bgu3)


def _down_body(se, st, sri, sto, sro, first, slot, en, tn_, hn, half_blk, n_steps, a_ref, w_hbm, bd_ref, o_ref,
               wf, wdb, sem):
    s = pl.program_id(0)
    tn = wdb.shape[1]
    bm = a_ref.shape[0]

    @pl.when(s >= n_steps[0])
    def _():
        o_ref[...] = jnp.zeros_like(o_ref)

    @pl.when(s < n_steps[0])
    def _():
        def copies(e, j, sl):
            return [pltpu.make_async_copy(w_hbm.at[e, :, pl.ds(pl.multiple_of(j * tn, tn), tn)],
                                          wf.at[sl], sem.at[sl])]

        def convert(sl):
            wdb[...] = wf[sl].astype(BF16)

        _weight_stream(s, first, slot, se, st, en, tn_, hn, copies, convert)

        @pl.when(half_blk[s] == 0)
        def _():
            o_ref[...] = _dot(a_ref[...], wdb[...]) + bd_ref[...]

        @pl.when(half_blk[s] == 1)
        def _():
            o_ref[0:bm // 2, :] = _dot(a_ref[0:bm // 2, :], wdb[...]) + bd_ref[...]
            o_ref[bm // 2:, :] = jnp.zeros((bm - bm // 2, tn), F32)


def _down(act, steps, wd, bd, tn):
    cap, de = act.shape
    e, _, d = wd.shape
    bm = MOE_ROWS
    grid_spec = pltpu.PrefetchScalarGridSpec(
        num_scalar_prefetch=N_STEP_ARRAYS,
        grid=((d // tn) * (cap // bm),),
        in_specs=[pl.BlockSpec((bm, de), lambda s, se, st, sri, *_: (sri[s], 0)),
                  pl.BlockSpec(memory_space=pl.ANY),
                  pl.BlockSpec((None, 1, tn), lambda s, se, st, *_: (se[s], 0, st[s]))],
        out_specs=pl.BlockSpec((bm, tn), lambda s, se, st, sri, sto, sro, *_: (sro[s], sto[s])),
        scratch_shapes=[pltpu.VMEM((2, de, tn), F32), pltpu.VMEM((de, tn), BF16), pltpu.SemaphoreType.DMA((2,))],
    )
    return pl.pallas_call(
        _down_body,
        grid_spec=grid_spec,
        out_shape=jax.ShapeDtypeStruct((cap, d), F32),
        compiler_params=_cp("arbitrary"),
        name="moe_down",
    )(*steps, act, wd, bd.reshape(e, 1, d))


def _combine_body(tm, n_tiles, dest_ref, dest_next_ref, rows_ref, gates_ref, x_ref, g2_ref, nw_ref, o_ref,
                  buf, sem):
    i = pl.program_id(0)
    slot = i % 2

    def fetch(dref, sl):
        def issue(r, carry_):
            for kk in range(TOP_K):
                d = dref[0, r * TOP_K + kk]
                _row_copy(rows_ref.at[pl.ds(d, 1)], buf.at[sl, kk, pl.ds(r, 1)], sem.at[sl]).start(priority=kk % 2)
            return carry_

        lax.fori_loop(0, tm, issue, 0)

    @pl.when(i == 0)
    def _():
        fetch(dest_ref, 0)

    @pl.when(i + 1 < n_tiles)
    def _():
        fetch(dest_next_ref, 1 - slot)

    def drain(r, carry_):
        for kk in range(TOP_K):
            _row_copy(rows_ref.at[pl.ds(0, 1)], buf.at[slot, kk, pl.ds(0, 1)], sem.at[slot]).wait()
        return carry_

    lax.fori_loop(0, tm, drain, 0)
    gates = gates_ref[...]
    y = gates[:, 0:1] * buf[slot, 0]
    for kk in range(1, TOP_K):
        y = y + gates[:, kk:kk + 1] * buf[slot, kk]
    x2 = x_ref[...] + g2_ref[...] * y
    o_ref[...] = _rms(x2, nw_ref[...])


def _combine(out_rows, dest, gates, x1, gate2, normf_w, seq):
    t, d = x1.shape
    tm = min(128, seq)
    tps = seq // tm
    n_tiles = t // tm
    dest3 = dest.reshape(n_tiles, 1, tm * TOP_K)
    return pl.pallas_call(
        functools.partial(_combine_body, tm, n_tiles),
        grid=(n_tiles,),
        in_specs=[pl.BlockSpec((None, 1, tm * TOP_K), lambda i: (i, 0, 0), memory_space=pltpu.SMEM),
                  pl.BlockSpec((None, 1, tm * TOP_K), lambda i: (jnp.minimum(i + 1, n_tiles - 1), 0, 0),
                               memory_space=pltpu.SMEM),
                  pl.BlockSpec(memory_space=pl.ANY),
                  pl.BlockSpec((tm, LANES), lambda i: (i, 0)),
                  pl.BlockSpec((tm, d), lambda i: (i, 0)),
                  pl.BlockSpec((None, 1, d), lambda i: (i // tps, 0, 0)),
                  pl.BlockSpec((1, d), lambda i: (0, 0))],
        out_specs=pl.BlockSpec((tm, d), lambda i: (i, 0)),
        out_shape=jax.ShapeDtypeStruct((t, d), F32),
        scratch_shapes=[pltpu.VMEM((2, TOP_K, tm, d), F32), pltpu.SemaphoreType.DMA((2,))],
        compiler_params=_cp("arbitrary"),
        name="moe_combine",
    )(dest3, dest3, out_rows, gates, x1, gate2, normf_w.reshape(1, d))


def _moe_plan(idx, rank, counts, n_experts):
    bm = MOE_ROWS
    n_assign = idx.shape[0] * TOP_K
    nblk = (counts + bm - 1) // bm
    blk_end = jnp.cumsum(nblk)
    blk_start = blk_end - nblk
    dest = (bm * blk_start[idx] + rank).astype(jnp.int32)
    nb = (n_assign + n_experts * (bm - 1) + bm - 1) // bm
    n_active = blk_end[-1].astype(jnp.int32)
    gaps = jnp.stack([bm * blk_start + counts, bm * nblk - counts, jnp.full_like(counts, n_active)]).astype(jnp.int32)
    return dest, gaps, blk_start.astype(jnp.int32), nblk.astype(jnp.int32), n_active, nb


def kernel(x, c, positions, w_ada, b_ada, norm1_w, w_in, rwkv_mu_rkvg, rwkv_mu_wa, rwkv_w0, rwkv_w1, rwkv_w2, rwkv_a0, rwkv_a1, rwkv_a2, rwkv_k_k, rwkv_k_a, rwkv_r_k, rwkv_lnx_w, rwkv_lnx_b, w_out, norm2_w, w_router, b_router, w_gate_up, b_gate_up, w_down, b_down, normf_w):
    bsz, seq, d = x.shape
    t = bsz * seq
    depth = w_ada.shape[0]
    rw = rwkv_w0.shape[1]
    ret_w = d - rw
    n_experts = w_router.shape[2]
    x2d = x.reshape(t, d)
    for l in range(depth):
        mod = _adaln(c, w_ada[l], b_ada[l])
        shift1, scale1, gate1, shift2, scale2, gate2 = [m.reshape(bsz, 1, d) for m in jnp.split(mod, 6, axis=-1)]
        h, tw, la = _norm1_lora(x2d, norm1_w[l].reshape(1, d), scale1, shift1,
                                rwkv_mu_wa[l, 0].reshape(1, d), rwkv_mu_wa[l, 1].reshape(1, d),
                                rwkv_w1[l].astype(BF16), rwkv_a1[l].astype(BF16), seq)
        p = _matmul(h, w_in[l].astype(BF16))
        mu = rwkv_mu_rkvg[l].reshape(4, 1, rw)
        prm = (mu[0], mu[1], mu[2], mu[3], rwkv_w0[l].reshape(1, rw), rwkv_a0[l].reshape(1, rw),
               rwkv_k_k[l].reshape(1, rw), rwkv_k_a[l].reshape(1, rw), rwkv_r_k[l].reshape(1, rw),
               rwkv_lnx_w[l].reshape(1, rw), rwkv_lnx_b[l].reshape(1, rw))
        y_rwkv = _rwkv(p, tw, la, prm, rwkv_w2[l].astype(BF16), rwkv_a2[l].astype(BF16), bsz, seq, rw)
        y_ret = _retention(p, positions, bsz, seq, 4 * rw, ret_w)
        wo = w_out[l].astype(BF16)
        x2d = _outproj(y_rwkv, y_ret, wo[:rw], wo[rw:], x2d, gate1, seq)
        h2, idx, gates, rank, counts = _router(x2d, norm2_w[l].reshape(1, d), scale2, shift2,
                                               w_router[l], b_router[l], seq)
        counts = counts[0, :n_experts]
        dest, gaps, blk_start, nblk, n_active, nb = _moe_plan(idx[:, :TOP_K], rank[:, :TOP_K], counts, n_experts)
        rows = _dispatch(h2, dest, gaps, nb * MOE_ROWS)
        de = w_down.shape[2]
        up_tiles = de // min(MOE_UP_TILE, de)
        act = _gate_up(rows, _moe_steps(counts, blk_start, nblk, n_active, up_tiles, nb), w_gate_up[l], b_gate_up[l])
        tn_down = min(MOE_DOWN_TILE, d)
        out_rows = _down(act, _moe_steps(counts, blk_start, nblk, n_active, d // tn_down, nb), w_down[l], b_down[l], tn_down)
        if l + 1 < depth:
            raise NotImplementedError("the final norm is fused into the last layer's combine")
        out = _combine(out_rows, dest, gates, x2d, gate2, normf_w, seq)
    return out.reshape(bsz, seq, d)
```

```python
import functools

import jax
import jax.numpy as jnp
from jax import lax
from jax.experimental import pallas as pl
from jax.experimental.pallas import tpu as pltpu

F32 = jnp.float32
BF16 = jnp.bfloat16

RWKV_HEAD = 64
RET_HEAD = 256
RET_CHUNK = 128
ROPE_BASE = 10000.0
TOP_K = 4
SWIGLU_LIMIT = 7.0
SWIGLU_ALPHA = 1.702
NORM_EPS = 1e-5
RWKV_GN_EPS = 64e-5
RET_GN_EPS = 1e-6

LANES = 128
SUBLANES = 8
RWKV_CHUNK = 64
RWKV_GROUP = 256
HEADS_PER_GROUP = RWKV_GROUP // RWKV_HEAD
MOE_ROWS = 512
MOE_UP_TILE = 256
MOE_DOWN_TILE = 2048
VMEM_LIMIT = 56 << 20


def _cp(*sem):
    return pltpu.CompilerParams(dimension_semantics=sem, vmem_limit_bytes=VMEM_LIMIT)


def _dot(a, b):
    return jnp.dot(a, b, preferred_element_type=F32)


def _dot_nt(a, b):
    return lax.dot_general(a, b, (((1,), (1,)), ((), ())), preferred_element_type=F32)


def _dot_tn(a, b):
    return lax.dot_general(a, b, (((0,), (0,)), ((), ())), preferred_element_type=F32)


def _split2(x):
    hi = x.astype(BF16)
    lo = (x - hi.astype(F32)).astype(BF16)
    return hi, lo


def _rms(x, w):
    return x * lax.rsqrt(jnp.mean(x * x, axis=-1, keepdims=True) + NORM_EPS) * w


def _adaln_body(c_ref, w_ref, b_ref, o_ref):
    c = c_ref[...]
    ca = c * jax.nn.sigmoid(c)
    o_ref[...] = _dot(ca.astype(BF16), w_ref[...].astype(BF16)) + b_ref[...]


def _adaln(c, w_ada, b_ada):
    bsz, d = c.shape
    n = w_ada.shape[1]
    tn = min(512, n)
    c8 = jnp.zeros((8, d), F32).at[:bsz].set(c)
    out = pl.pallas_call(
        _adaln_body,
        grid=(n // tn,),
        in_specs=[pl.BlockSpec((8, d), lambda j: (0, 0)),
                  pl.BlockSpec((d, tn), lambda j: (0, j)),
                  pl.BlockSpec((1, tn), lambda j: (0, j))],
        out_specs=pl.BlockSpec((8, tn), lambda j: (0, j)),
        out_shape=jax.ShapeDtypeStruct((8, n), F32),
        compiler_params=_cp("arbitrary"),
        name="adaln",
    )(c8, w_ada, b_ada.reshape(1, n))
    return out[:bsz]


def _norm1_body(tiles_per_seq, x_ref, nw_ref, sc_ref, sh_ref, muw_ref, mua_ref, w1_ref, a1_ref,
                h_ref, tw_ref, la_ref, carry):
    i = pl.program_id(0)

    @pl.when(i % tiles_per_seq == 0)
    def _():
        carry[...] = jnp.zeros_like(carry)

    x = x_ref[...]
    tm = x.shape[0]
    h = _rms(x, nw_ref[...]) * (1.0 + sc_ref[...]) + sh_ref[...]
    row = lax.broadcasted_iota(jnp.int32, h.shape, 0)
    h_prev = jnp.where(row == 0, carry[0:1, :], pltpu.roll(h, 1, 0))
    carry[0:1, :] = h[tm - 1:tm, :]
    dh = h_prev - h
    xw = h + dh * muw_ref[...]
    xa = h + dh * mua_ref[...]
    tw_ref[...] = jnp.tanh(_dot(xw.astype(BF16), w1_ref[...]))
    la_ref[...] = _dot(xa.astype(BF16), a1_ref[...])
    h_ref[...] = h.astype(BF16)


def _norm1_lora(x2d, nw, scale, shift, mu_w, mu_a, w1b, a1b, seq):
    t, d = x2d.shape
    tm = min(512, seq)
    tps = seq // tm
    r = w1b.shape[1]
    row_d = pl.BlockSpec((1, d), lambda i: (0, 0))
    mod = pl.BlockSpec((None, 1, d), lambda i: (i // tps, 0, 0))
    lora = pl.BlockSpec((d, r), lambda i: (0, 0))
    return pl.pallas_call(
        functools.partial(_norm1_body, tps),
        grid=(t // tm,),
        in_specs=[pl.BlockSpec((tm, d), lambda i: (i, 0)), row_d, mod, mod, row_d, row_d, lora, lora],
        out_specs=[pl.BlockSpec((tm, d), lambda i: (i, 0)),
                   pl.BlockSpec((tm, r), lambda i: (i, 0)),
                   pl.BlockSpec((tm, r), lambda i: (i, 0))],
        out_shape=[jax.ShapeDtypeStruct((t, d), BF16),
                   jax.ShapeDtypeStruct((t, r), F32),
                   jax.ShapeDtypeStruct((t, r), F32)],
        scratch_shapes=[pltpu.VMEM((8, d), F32)],
        compiler_params=_cp("arbitrary"),
        name="norm1_lora",
    )(x2d, nw, scale, shift, mu_w, mu_a, w1b, a1b)


def _matmul_body(a_ref, b_ref, o_ref):
    o_ref[...] = _dot(a_ref[...], b_ref[...])


def _matmul(a, b):
    m, k = a.shape
    n = b.shape[1]
    tm, tn = min(1024, m), min(512, n)
    return pl.pallas_call(
        _matmul_body,
        grid=(m // tm, n // tn),
        in_specs=[pl.BlockSpec((tm, k), lambda i, j: (i, 0)),
                  pl.BlockSpec((k, tn), lambda i, j: (0, j))],
        out_specs=pl.BlockSpec((tm, tn), lambda i, j: (i, j)),
        out_shape=jax.ShapeDtypeStruct((m, n), F32),
        compiler_params=_cp("parallel", "arbitrary"),
        name="in_proj",
    )(a, b)


def _rwkv_body(tb, pr_ref, pk_ref, pv_ref, pg_ref, tw_ref, la_ref,
               mur_ref, muk_ref, muv_ref, mug_ref, w0_ref, a0_ref, kk_ref, ka_ref, rk_ref,
               lnw_ref, lnb_ref, w2_ref, a2_ref,
               bd_ref, mstk_ref, sm_ref, im_ref, tri_ref, eye_ref, lvl_ref,
               y_ref,
               carry, state):
    i = pl.program_id(2)
    c = RWKV_CHUNK

    @pl.when(i == 0)
    def _():
        carry[...] = jnp.zeros_like(carry)
        state[...] = jnp.zeros_like(state)

    row = lax.broadcasted_iota(jnp.int32, (tb, RWKV_GROUP), 0)

    def shift_mix(ref, slot, mu):
        cur = ref[...]
        prev = jnp.where(row == 0, carry[slot:slot + 1, :], pltpu.roll(cur, 1, 0))
        carry[slot:slot + 1, :] = cur[tb - 1:tb, :]
        return cur + (prev - cur) * mu

    bd = bd_ref[...]

    def seg_sum(x):
        hi, lo = _split2(x)
        return _dot(hi, bd) + _dot(lo, bd)

    r = shift_mix(pr_ref, 0, mur_ref[...])
    k = shift_mix(pk_ref, 1, muk_ref[...])
    v = shift_mix(pv_ref, 2, muv_ref[...])
    g = shift_mix(pg_ref, 3, mug_ref[...])

    z = -(w0_ref[...] + _dot(tw_ref[...].astype(BF16), w2_ref[...]))
    softplus = jnp.maximum(z, 0.0) + jnp.log(1.0 + jnp.exp(-jnp.abs(z)))
    lw = -jnp.exp(-softplus - 0.5)
    a_icl = jax.nn.sigmoid(a0_ref[...] + _dot(la_ref[...].astype(BF16), a2_ref[...]))
    kk = k * kk_ref[...]
    kk = kk * lax.rsqrt(jnp.maximum(seg_sum(kk * kk), 1e-24))
    k = k * (1.0 + (a_icl - 1.0) * ka_ref[...])
    a_in = -kk
    b_in = kk * a_icl

    mstk = mstk_ref[...]
    strict = sm_ref[...] > 0.0
    incl = im_ref[...] > 0.0
    same_head = bd > 0
    tri = tri_ref[...]

    def stack(x):
        return jnp.concatenate([x.astype(BF16)] * HEADS_PER_GROUP, axis=0) * mstk

    n_ch = tb // c

    def each(fn, *cols):
        return [fn(*args) for args in zip(*cols)]

    rows = [slice(ci * c, (ci + 1) * c) for ci in range(n_ch)]
    lw_c = [lw[sl] for sl in rows]

    def cumsum(x):
        hi, lo = _split2(x)
        return _dot(tri, hi) + _dot(tri, lo)

    cum = each(cumsum, lw_c)
    cum_end = [x[c - 1:c, :] for x in cum]
    p_inv = each(lambda x: jnp.exp(-x), cum)
    p_end = each(lambda x, e: jnp.exp(e - x), cum, cum_end)
    a_d = each(lambda sl, x, l: (a_in[sl] * jnp.exp(x - l)).astype(BF16), rows, cum, lw_c)
    r_d = each(lambda sl, x: r[sl] * jnp.exp(x), rows, cum)
    ar_b = each(lambda x, y: jnp.concatenate([x, y.astype(BF16)], axis=0), a_d, r_d)
    a_t = each(stack, a_d)
    b_t = each(lambda sl, p: stack(b_in[sl] * p), rows, p_inv)
    k_t = each(lambda sl, p: stack(k[sl] * p), rows, p_inv)
    v_t = each(lambda sl: stack(v[sl]), rows)
    b_h = each(lambda sl, p: (b_in[sl] * p).astype(BF16), rows, p_end)
    uk_r = each(lambda sl, p: (k[sl] * p).astype(BF16), rows, p_end)
    by_b = each(_dot_nt, ar_b, b_t)
    by_k = each(_dot_nt, ar_b, k_t)
    a_ab = each(lambda x: jnp.where(strict, x[0:c], 0.0), by_b)
    a_ak = each(lambda x: jnp.where(strict, x[0:c], 0.0).astype(BF16), by_k)
    a_rb = each(lambda x: jnp.where(incl, x[c:], 0.0).astype(BF16), by_b)
    a_rk = each(lambda x: jnp.where(incl, x[c:], 0.0).astype(BF16), by_k)
    inv = each(lambda x: eye_ref[...] + x * lvl_ref[0], a_ab)
    for lv in range(1, c.bit_length() - 1):
        inv_bd = each(stack, inv)
        w = each(lambda x, t: _dot((x * lvl_ref[lv]).astype(BF16), t), a_ab, inv_bd)
        inv = each(lambda x, ww: x + _dot(x.astype(BF16), stack(ww)), inv, w)
    invb = each(lambda x: x.astype(BF16), inv)
    a_p = each(_dot, invb, a_t)
    akv = each(_dot, a_ak, v_t)
    u_loc = each(lambda t, x: _dot(t, stack(x)), invb, akv)
    r_p = each(lambda x, m, y: (x + _dot(m, stack(y))).astype(BF16), r_d, a_rb, a_p)
    y_loc = each(lambda m, x, n, y: _dot(m, stack(x)) + _dot(n, y), a_rb, u_loc, a_rk, v_t)
    phi = each(lambda x, y: jnp.where(same_head, _dot_tn(x.astype(BF16), y), 0.0).astype(BF16), a_p, b_h)
    s_loc = each(lambda x, y, sl, z: jnp.where(
        same_head,
        _dot_tn(jnp.concatenate([x.astype(BF16), v[sl].astype(BF16)], axis=0), jnp.concatenate([y, z], axis=0)),
        0.0), u_loc, b_h, rows, uk_r)

    s = state[...]
    ys = []
    for ci in range(n_ch):
        sb = s.astype(BF16)
        ys.append(_dot_nt(r_p[ci], sb) + y_loc[ci])
        s = s * jnp.exp(cum_end[ci]) + _dot(sb, phi[ci]) + s_loc[ci]
    state[...] = s

    y = jnp.concatenate(ys, axis=0)
    inv_n = 1.0 / RWKV_HEAD
    mean = seg_sum(y) * inv_n
    dlt = y - mean
    var = seg_sum(dlt * dlt) * inv_n
    yn = dlt * lax.rsqrt(var + RWKV_GN_EPS) * lnw_ref[...] + lnb_ref[...]
    bonus = seg_sum(r * k * rk_ref[...]) * v
    y_ref[...] = ((yn + bonus) * jax.nn.sigmoid(g)).astype(BF16)


def _rwkv_consts():
    n = RWKV_GROUP
    c = RWKV_CHUNK
    lane = jnp.arange(n)
    head = lane // RWKV_HEAD
    blk = lane // c
    bd = (head[:, None] == head[None, :]).astype(BF16)
    mstk = (blk[:, None] == head[None, :]).astype(BF16)
    t_idx = jnp.arange(c)[:, None]
    s_idx = (lane % c)[None, :]
    strict = (s_idx < t_idx).astype(F32)
    incl = (s_idx <= t_idx).astype(F32)
    eye = (s_idx == t_idx).astype(F32)
    tri = (jnp.arange(c)[None, :] <= jnp.arange(c)[:, None]).astype(BF16)
    lvls = []
    for lv in range(c.bit_length() - 1):
        m = 1 << lv
        hi_lo = ((t_idx // m) % 2 == 1) & ((s_idx // m) % 2 == 0)
        lvls.append((hi_lo & (t_idx // (2 * m) == s_idx // (2 * m))).astype(F32))
    return bd, mstk, strict, incl, tri, eye, jnp.stack(lvls)


def _rwkv(p, tw, la, prm, w2b, a2b, bsz, seq, rw):
    t = p.shape[0]
    g = rw // RWKV_GROUP
    tb = min(1024, seq)
    nt = seq // tb
    r = tw.shape[1]
    bd, mstk, strict, incl, tri, eye, lvls = _rwkv_consts()

    def pcol(m):
        return pl.BlockSpec((tb, RWKV_GROUP), lambda b, gg, i: (b * nt + i, m * g + gg))

    lor = pl.BlockSpec((tb, r), lambda b, gg, i: (b * nt + i, 0))
    chan = pl.BlockSpec((1, RWKV_GROUP), lambda b, gg, i: (0, gg))
    lw2 = pl.BlockSpec((r, RWKV_GROUP), lambda b, gg, i: (0, gg))
    sq = pl.BlockSpec((RWKV_GROUP, RWKV_GROUP), lambda b, gg, i: (0, 0))
    tri_spec = pl.BlockSpec((RWKV_CHUNK, RWKV_CHUNK), lambda b, gg, i: (0, 0))
    wide = pl.BlockSpec((RWKV_CHUNK, RWKV_GROUP), lambda b, gg, i: (0, 0))
    return pl.pallas_call(
        functools.partial(_rwkv_body, tb),
        grid=(bsz, g, nt),
        in_specs=[pcol(0), pcol(1), pcol(2), pcol(3), lor, lor] + [chan] * 11 + [lw2, lw2]
                 + [sq, sq, wide, wide, tri_spec, wide, pl.BlockSpec(lvls.shape, lambda b, gg, i: (0, 0, 0))],
        out_specs=pl.BlockSpec((tb, RWKV_GROUP), lambda b, gg, i: (b * nt + i, gg)),
        out_shape=jax.ShapeDtypeStruct((t, rw), BF16),
        scratch_shapes=[pltpu.VMEM((8, RWKV_GROUP), F32), pltpu.VMEM((RWKV_GROUP, RWKV_GROUP), F32)],
        compiler_params=_cp("parallel", "parallel", "arbitrary"),
        name="rwkv7",
    )(p, p, p, p, tw, la, *prm, w2b, a2b, bd, mstk, strict, incl, tri, eye, lvls)


def _ret_body(heads, q_ref, k_ref, v_ref, g_ref, pos_ref, invf_ref, inner_ref, qd_ref, kd_ref, cd_ref,
              o_ref, state):
    @pl.when(pl.program_id(1) == 0)
    def _():
        state[...] = jnp.zeros_like(state)

    half = RET_HEAD // 2
    ang = pos_ref[...].astype(F32) * invf_ref[...]
    cos = jnp.cos(ang)
    sin = jnp.sin(ang)

    def rope(x):
        x1, x2 = x[:, :half], x[:, half:]
        return jnp.concatenate([x1 * cos - x2 * sin, x1 * sin + x2 * cos], axis=-1)

    for hh in range(heads):
        sl = slice(hh * RET_HEAD, (hh + 1) * RET_HEAD)
        q = rope(q_ref[:, sl])
        k = rope(k_ref[:, sl]) * (RET_HEAD ** -0.5)
        v = v_ref[:, sl].astype(BF16)
        g = g_ref[:, sl]
        qb = q.astype(BF16)
        scores = _dot_nt(qb, k.astype(BF16)) * inner_ref[hh]
        st = state[hh]
        o = _dot(scores.astype(BF16), v) + _dot(qb, st.astype(BF16)) * qd_ref[hh]
        state[hh] = st * cd_ref[hh] + _dot_tn((k * kd_ref[hh]).astype(BF16), v)
        on = o * lax.rsqrt(jnp.mean(o * o, axis=-1, keepdims=True) + RET_GN_EPS)
        o_ref[:, sl] = (on * (g * jax.nn.sigmoid(g))).astype(BF16)


def _retention(p, positions, bsz, seq, col0, width):
    t = p.shape[0]
    heads = width // RET_HEAD
    c = RET_CHUNK
    nc = seq // c
    half = RET_HEAD // 2
    cb = col0 // width
    inv_freq = (ROPE_BASE ** (-jnp.arange(half, dtype=F32) / half)).reshape(1, half)
    log_gamma = jnp.log(1.0 - 2.0 ** (-5.0 - jnp.arange(heads, dtype=F32)))
    idx = jnp.arange(c, dtype=F32)
    diff = idx[:, None] - idx[None, :]
    inner = jnp.where(diff >= 0, jnp.exp(log_gamma[:, None, None] * jnp.maximum(diff, 0.0)), 0.0)
    q_decay = jnp.exp(log_gamma[:, None] * (idx + 1.0))[:, :, None]
    k_decay = jnp.exp(log_gamma[:, None] * (c - 1.0 - idx))[:, :, None]
    chunk_decay = jnp.broadcast_to(jnp.exp(log_gamma * c)[:, None, None], (heads, 1, RET_HEAD))

    def pcol(m):
        return pl.BlockSpec((c, width), lambda b, i: (b * nc + i, cb + m))

    def full(a):
        return pl.BlockSpec(a.shape, lambda b, i: (0,) * a.ndim)

    consts = (inv_freq, inner, q_decay, k_decay, chunk_decay)
    return pl.pallas_call(
        functools.partial(_ret_body, heads),
        grid=(bsz, nc),
        in_specs=[pcol(0), pcol(1), pcol(2), pcol(3),
                  pl.BlockSpec((c, 1), lambda b, i: (b * nc + i, 0))] + [full(a) for a in consts],
        out_specs=pl.BlockSpec((c, width), lambda b, i: (b * nc + i, 0)),
        out_shape=jax.ShapeDtypeStruct((t, width), BF16),
        scratch_shapes=[pltpu.VMEM((heads, RET_HEAD, RET_HEAD), F32)],
        compiler_params=_cp("parallel", "arbitrary"),
        name="retention",
    )(p, p, p, p, positions.reshape(t, 1), *consts)


def _outproj_body(ya_ref, yb_ref, wa_ref, wb_ref, x_ref, g_ref, o_ref):
    mix = _dot(ya_ref[...], wa_ref[...]) + _dot(yb_ref[...], wb_ref[...])
    o_ref[...] = x_ref[...] + g_ref[...] * mix


def _outproj(ya, yb, wa, wb, x2d, gate, seq):
    t, d = x2d.shape
    ka, kb = ya.shape[1], yb.shape[1]
    tm, tn = min(1024, seq), min(512, d)
    tps = seq // tm
    return pl.pallas_call(
        _outproj_body,
        grid=(t // tm, d // tn),
        in_specs=[pl.BlockSpec((tm, ka), lambda i, j: (i, 0)),
                  pl.BlockSpec((tm, kb), lambda i, j: (i, 0)),
                  pl.BlockSpec((ka, tn), lambda i, j: (0, j)),
                  pl.BlockSpec((kb, tn), lambda i, j: (0, j)),
                  pl.BlockSpec((tm, tn), lambda i, j: (i, j)),
                  pl.BlockSpec((None, 1, tn), lambda i, j: (i // tps, 0, j))],
        out_specs=pl.BlockSpec((tm, tn), lambda i, j: (i, j)),
        out_shape=jax.ShapeDtypeStruct((t, d), F32),
        compiler_params=_cp("parallel", "arbitrary"),
        name="out_proj",
    )(ya, yb, wa, wb, x2d, gate)


def _pack_bf16_pairs(h):
    half = h.shape[1] // 2
    hi = lax.bitcast_convert_type(h[:, :half].astype(BF16).astype(F32), jnp.int32)
    lo = lax.bitcast_convert_type(h[:, half:].astype(BF16).astype(F32), jnp.int32)
    return hi | lax.shift_right_logical(lo, 16)


def _unpack_bf16_pairs(w):
    hi = lax.bitcast_convert_type(w & -65536, F32).astype(BF16)
    lo = lax.bitcast_convert_type(w << 16, F32).astype(BF16)
    return hi, lo


def _router_body(x_ref, nw_ref, sc_ref, sh_ref, wrh_ref, wrl_ref, br_ref, tri_ref,
                 h_ref, idx_ref, gate_ref, rank_ref, cnt_ref, carry):
    @pl.when(pl.program_id(0) == 0)
    def _():
        carry[...] = jnp.zeros_like(carry)

    h = _rms(x_ref[...], nw_ref[...]) * (1.0 + sc_ref[...]) + sh_ref[...]
    h_ref[...] = _pack_bf16_pairs(h)
    h_hi, h_lo = _split2(h)
    logits = _dot(h_hi, wrh_ref[...]) + _dot(h_hi, wrl_ref[...]) + _dot(h_lo, wrh_ref[...]) + br_ref[...]
    lane = lax.broadcasted_iota(jnp.int32, logits.shape, 1)
    vals, idxs = [], []
    rest = logits
    for _ in range(TOP_K):
        m = jnp.max(rest, axis=-1, keepdims=True)
        ix = jnp.min(jnp.where(rest == m, lane, LANES), axis=-1, keepdims=True)
        vals.append(m)
        idxs.append(ix)
        rest = jnp.where(lane == ix, -jnp.inf, rest)
    exps = [jnp.exp(vv - vals[0]) for vv in vals]
    den = exps[0]
    for e in exps[1:]:
        den = den + e
    hot = (lane == idxs[0]).astype(F32)
    for ix in idxs[1:]:
        hot = hot + (lane == ix).astype(F32)
    before = _dot(tri_ref[...], hot.astype(BF16)) + carry[0:1, :]
    carry[0:1, :] = carry[0:1, :] + jnp.sum(hot, axis=0, keepdims=True)
    idx_out = jnp.zeros(logits.shape, jnp.int32)
    gate_out = jnp.zeros(logits.shape, F32)
    rank_out = jnp.zeros(logits.shape, F32)
    for kk in range(TOP_K):
        rk = jnp.sum(jnp.where(lane == idxs[kk], before, 0.0), axis=-1, keepdims=True)
        idx_out = jnp.where(lane == kk, idxs[kk], idx_out)
        gate_out = jnp.where(lane == kk, exps[kk] / den, gate_out)
        rank_out = jnp.where(lane == kk, rk, rank_out)
    idx_ref[...] = idx_out
    gate_ref[...] = gate_out
    rank_ref[...] = rank_out.astype(jnp.int32)
    cnt_ref[...] = jnp.broadcast_to(carry[0:1, :], cnt_ref.shape).astype(jnp.int32)


def _router(x2d, nw, scale, shift, w_router, b_router, seq):
    t, d = x2d.shape
    e = w_router.shape[1]
    tm = min(512, seq)
    tps = seq // tm
    wr = jnp.zeros((d, LANES), F32).at[:, :e].set(w_router)
    wr_hi, wr_lo = _split2(wr)
    br = jnp.full((1, LANES), -1e30, F32).at[0, :e].set(b_router)
    tri = (jnp.arange(tm)[None, :] < jnp.arange(tm)[:, None]).astype(BF16)
    row_d = pl.BlockSpec((1, d), lambda i: (0, 0))
    mod = pl.BlockSpec((None, 1, d), lambda i: (i // tps, 0, 0))
    lane_out = pl.BlockSpec((tm, LANES), lambda i: (i, 0))
    return pl.pallas_call(
        _router_body,
        grid=(t // tm,),
        in_specs=[pl.BlockSpec((tm, d), lambda i: (i, 0)), row_d, mod, mod,
                  pl.BlockSpec((d, LANES), lambda i: (0, 0)),
                  pl.BlockSpec((d, LANES), lambda i: (0, 0)),
                  pl.BlockSpec((1, LANES), lambda i: (0, 0)),
                  pl.BlockSpec((tm, tm), lambda i: (0, 0))],
        out_specs=[pl.BlockSpec((tm, d // 2), lambda i: (i, 0)), lane_out, lane_out, lane_out,
                   pl.BlockSpec((8, LANES), lambda i: (0, 0))],
        out_shape=[jax.ShapeDtypeStruct((t, d // 2), jnp.int32),
                   jax.ShapeDtypeStruct((t, LANES), jnp.int32),
                   jax.ShapeDtypeStruct((t, LANES), F32),
                   jax.ShapeDtypeStruct((t, LANES), jnp.int32),
                   jax.ShapeDtypeStruct((8, LANES), jnp.int32)],
        scratch_shapes=[pltpu.VMEM((8, LANES), F32)],
        compiler_params=_cp("arbitrary"),
        name="norm2_router",
    )(x2d, nw, scale, shift, wr_hi, wr_lo, br, tri)


def _row_copy(src, dst, sem):
    return pltpu.make_async_copy(src, dst, sem)


def _dispatch_body(tm, gaps_ref, dest_ref, h_ref, rows_out, zeros, sem, zsem):
    zr, half = zeros.shape
    n_experts = gaps_ref.shape[1]

    def zero_fill(start_or_wait):
        def gap(e, carry_):
            pos = gaps_ref[0, e]
            length = gaps_ref[1, e]
            head = jnp.minimum((-pos) % SUBLANES, length)
            n_tiles = (length - head) // SUBLANES
            body = pos + head
            tail = body + n_tiles * SUBLANES

            def row(at):
                def one(r, c2):
                    start_or_wait(pltpu.make_async_copy(zeros.at[pl.ds(0, 1)], rows_out.at[pl.ds(at + r, 1)], zsem))
                    return c2
                return one

            def tile(r, c2):
                dst = rows_out.at[pl.ds(pl.multiple_of(body + r * SUBLANES, SUBLANES), SUBLANES)]
                start_or_wait(pltpu.make_async_copy(zeros.at[pl.ds(0, SUBLANES)], dst, zsem))
                return c2

            lax.fori_loop(0, head, row(pos), 0)
            lax.fori_loop(0, n_tiles, tile, 0)
            lax.fori_loop(0, length - head - n_tiles * SUBLANES, row(tail), 0)
            return carry_

        lax.fori_loop(0, n_experts, gap, 0)

        def spare(b, carry_):
            for hh in range(MOE_ROWS // zr):
                dst = rows_out.at[pl.ds(pl.multiple_of(b * MOE_ROWS + hh * zr, zr), zr)]
                start_or_wait(pltpu.make_async_copy(zeros, dst, zsem))
            return carry_

        lax.fori_loop(gaps_ref[2, 0], rows_out.shape[0] // MOE_ROWS, spare, 0)

    @pl.when(pl.program_id(0) == 0)
    def _():
        zeros[...] = jnp.zeros_like(zeros)
        zero_fill(lambda cp: cp.start())

    def issue(r, carry_):
        for kk in range(TOP_K):
            d = dest_ref[0, r * TOP_K + kk]
            _row_copy(h_ref.at[pl.ds(r, 1)], rows_out.at[pl.ds(d, 1)], sem).start(priority=kk % 2)
        return carry_

    def drain(r, carry_):
        for _ in range(TOP_K):
            _row_copy(h_ref.at[pl.ds(0, 1)], rows_out.at[pl.ds(0, 1)], sem).wait()
        return carry_

    lax.fori_loop(0, tm, issue, 0)
    lax.fori_loop(0, tm, drain, 0)

    @pl.when(pl.program_id(0) == 0)
    def _():
        zero_fill(lambda cp: cp.wait())


def _dispatch(h2, dest, gaps, cap):
    t, d = h2.shape
    tm = min(128, t)
    dest3 = dest.reshape(t // tm, 1, tm * TOP_K)
    return pl.pallas_call(
        functools.partial(_dispatch_body, tm),
        grid=(t // tm,),
        in_specs=[pl.BlockSpec(memory_space=pltpu.SMEM),
                  pl.BlockSpec((None, 1, tm * TOP_K), lambda i: (i, 0, 0), memory_space=pltpu.SMEM),
                  pl.BlockSpec((tm, d), lambda i: (i, 0))],
        out_specs=pl.BlockSpec(memory_space=pl.ANY),
        out_shape=jax.ShapeDtypeStruct((cap, d), h2.dtype),
        scratch_shapes=[pltpu.VMEM((MOE_ROWS // 2, d), h2.dtype), pltpu.SemaphoreType.DMA, pltpu.SemaphoreType.DMA],
        compiler_params=_cp("arbitrary"),
        name="moe_dispatch",
    )(gaps, dest3, h2)


def _moe_steps(counts, blk_start, nblk, n_active, n_tiles, nb):
    n_experts = blk_start.shape[0]
    total = n_tiles * nb
    s = jnp.arange(total, dtype=jnp.int32)
    step_end = n_tiles * (blk_start + nblk)
    e = jnp.minimum(jnp.sum(step_end[None, :] <= s[:, None], axis=1), n_experts - 1).astype(jnp.int32)
    local = s - n_tiles * blk_start[e]
    per = jnp.maximum(nblk[e], 1)
    tile = local // per
    rb = blk_start[e] + local % per
    first = local % per == 0
    n_steps = (n_tiles * n_active).astype(jnp.int32)
    active = s < n_steps
    last = n_steps - 1
    spare = s - n_steps
    slot = (jnp.cumsum(first.astype(jnp.int32)) - 1) % 2
    nxt = jnp.minimum(s - local % per + per, total - 1)
    cols = (jnp.where(active, e, e[last]), jnp.where(active, tile, tile[last]), jnp.where(active, rb, rb[last]),
            jnp.where(active, tile, spare % n_tiles), jnp.where(active, rb, n_active + spare // n_tiles),
            first, slot, e[nxt], tile[nxt], nxt < n_steps,
            counts[e] - MOE_ROWS * (local % per) <= MOE_ROWS // 2)
    return tuple(x.astype(jnp.int32) for x in cols) + (n_steps.reshape(1),)


N_STEP_ARRAYS = 12


def _weight_stream(s, first, slot, e_cur, t_cur, e_nxt, t_nxt, has_nxt, copies, convert):
    @pl.when(first[s] == 1)
    def _():
        sl = slot[s]

        @pl.when(s == 0)
        def _():
            for cp in copies(e_cur[0], t_cur[0], 0):
                cp.start()

        for cp in copies(e_cur[s], t_cur[s], sl):
            cp.wait()
        convert(sl)

        @pl.when(has_nxt[s] == 1)
        def _():
            for cp in copies(e_nxt[s], t_nxt[s], 1 - sl):
                cp.start()


def _gate_up_body(se, st, sri, sto, sro, first, slot, en, tn_, hn, half_blk, n_steps, x_ref, w_hbm, bg_ref, bu_ref,
                  act_ref, wf, wgb, wub, sem):
    s = pl.program_id(0)
    d, tn = wgb.shape
    nj = w_hbm.shape[2] // (2 * tn)
    bm = x_ref.shape[0]

    @pl.when(s >= n_steps[0])
    def _():
        act_ref[...] = jnp.zeros_like(act_ref)

    @pl.when(s < n_steps[0])
    def _():
        def copies(e, j, sl):
            return [pltpu.make_async_copy(w_hbm.at[e, :, pl.ds(pl.multiple_of((m * nj + j) * tn, tn), tn)],
                                          wf.at[sl, m], sem.at[sl]) for m in range(2)]

        def convert(sl):
            wgb[...] = wf[sl, 0].astype(BF16)
            wub[...] = wf[sl, 1].astype(BF16)

        _weight_stream(s, first, slot, se, st, en, tn_, hn, copies, convert)
        half = d // 2

        def rows(n):
            xa, xb = _unpack_bf16_pairs(x_ref[0:n, :])

            def proj(w, b_ref):
                return _dot(xa, w[:half, :]) + _dot(xb, w[half:, :]) + b_ref[...]

            gate = jnp.minimum(proj(wgb, bg_ref), SWIGLU_LIMIT)
            up = jnp.clip(proj(wub, bu_ref), -SWIGLU_LIMIT, SWIGLU_LIMIT)
            act_ref[0:n, :] = ((up + 1.0) * gate * jax.nn.sigmoid(SWIGLU_ALPHA * gate)).astype(BF16)

        @pl.when(half_blk[s] == 0)
        def _():
            rows(bm)

        @pl.when(half_blk[s] == 1)
        def _():
            rows(bm // 2)
            act_ref[bm // 2:, :] = jnp.zeros((bm - bm // 2, tn), BF16)


def _gate_up(rows, steps, wgu, bgu):
    cap, half = rows.shape
    d = 2 * half
    e, _, two_de = wgu.shape
    de = two_de // 2
    bm = MOE_ROWS
    tn = min(MOE_UP_TILE, de)
    nj = de // tn
    grid_spec = pltpu.PrefetchScalarGridSpec(
        num_scalar_prefetch=N_STEP_ARRAYS,
        grid=(nj * (cap // bm),),
        in_specs=[pl.BlockSpec((bm, half), lambda s, se, st, sri, *_: (sri[s], 0)),
                  pl.BlockSpec(memory_space=pl.ANY),
                  pl.BlockSpec((None, 1, tn), lambda s, se, st, *_: (se[s], 0, st[s])),
                  pl.BlockSpec((None, 1, tn), lambda s, se, st, *_: (se[s], 0, nj + st[s]))],
        out_specs=pl.BlockSpec((bm, tn), lambda s, se, st, sri, sto, sro, *_: (sro[s], sto[s])),
        scratch_shapes=[pltpu.VMEM((2, 2, d, tn), F32), pltpu.VMEM((d, tn), BF16), pltpu.VMEM((d, tn), BF16),
                        pltpu.SemaphoreType.DMA((2,))],
    )
    bgu3 = bgu.reshape(e, 1, two_de)
    return pl.pallas_call(
        _gate_up_body,
        grid_spec=grid_spec,
        out_shape=jax.ShapeDtypeStruct((cap, de), BF16),
        compiler_params=_cp("arbitrary"),
        name="moe_gate_up",
    )(*steps, rows, wgu, bgu3, bgu3)


def _down_body(se, st, sri, sto, sro, first, slot, en, tn_, hn, half_blk, n_steps, a_ref, w_hbm, bd_ref, o_ref,
               wf, wdb, sem):
    s = pl.program_id(0)
    tn = wdb.shape[1]
    bm = a_ref.shape[0]

    @pl.when(s >= n_steps[0])
    def _():
        o_ref[...] = jnp.zeros_like(o_ref)

    @pl.when(s < n_steps[0])
    def _():
        def copies(e, j, sl):
            return [pltpu.make_async_copy(w_hbm.at[e, :, pl.ds(pl.multiple_of(j * tn, tn), tn)],
                                          wf.at[sl], sem.at[sl])]

        def convert(sl):
            wdb[...] = wf[sl].astype(BF16)

        _weight_stream(s, first, slot, se, st, en, tn_, hn, copies, convert)

        @pl.when(half_blk[s] == 0)
        def _():
            o_ref[...] = _dot(a_ref[...], wdb[...]) + bd_ref[...]

        @pl.when(half_blk[s] == 1)
        def _():
            o_ref[0:bm // 2, :] = _dot(a_ref[0:bm // 2, :], wdb[...]) + bd_ref[...]
            o_ref[bm // 2:, :] = jnp.zeros((bm - bm // 2, tn), F32)


def _down(act, steps, wd, bd, tn):
    cap, de = act.shape
    e, _, d = wd.shape
    bm = MOE_ROWS
    grid_spec = pltpu.PrefetchScalarGridSpec(
        num_scalar_prefetch=N_STEP_ARRAYS,
        grid=((d // tn) * (cap // bm),),
        in_specs=[pl.BlockSpec((bm, de), lambda s, se, st, sri, *_: (sri[s], 0)),
                  pl.BlockSpec(memory_space=pl.ANY),
                  pl.BlockSpec((None, 1, tn), lambda s, se, st, *_: (se[s], 0, st[s]))],
        out_specs=pl.BlockSpec((bm, tn), lambda s, se, st, sri, sto, sro, *_: (sro[s], sto[s])),
        scratch_shapes=[pltpu.VMEM((2, de, tn), F32), pltpu.VMEM((de, tn), BF16), pltpu.SemaphoreType.DMA((2,))],
    )
    return pl.pallas_call(
        _down_body,
        grid_spec=grid_spec,
        out_shape=jax.ShapeDtypeStruct((cap, d), F32),
        compiler_params=_cp("arbitrary"),
        name="moe_down",
    )(*steps, act, wd, bd.reshape(e, 1, d))


def _combine_body(tm, n_tiles, dest_ref, dest_next_ref, rows_ref, gates_ref, x_ref, g2_ref, nw_ref, o_ref,
                  buf, sem):
    i = pl.program_id(0)
    slot = i % 2

    def fetch(dref, sl):
        def issue(r, carry_):
            for kk in range(TOP_K):
                d = dref[0, r * TOP_K + kk]
                _row_copy(rows_ref.at[pl.ds(d, 1)], buf.at[sl, kk, pl.ds(r, 1)], sem.at[sl]).start(priority=kk % 2)
            return carry_

        lax.fori_loop(0, tm, issue, 0)

    @pl.when(i == 0)
    def _():
        fetch(dest_ref, 0)

    @pl.when(i + 1 < n_tiles)
    def _():
        fetch(dest_next_ref, 1 - slot)

    def drain(r, carry_):
        for kk in range(TOP_K):
            _row_copy(rows_ref.at[pl.ds(0, 1)], buf.at[slot, kk, pl.ds(0, 1)], sem.at[slot]).wait()
        return carry_

    lax.fori_loop(0, tm, drain, 0)
    gates = gates_ref[...]
    y = gates[:, 0:1] * buf[slot, 0]
    for kk in range(1, TOP_K):
        y = y + gates[:, kk:kk + 1] * buf[slot, kk]
    x2 = x_ref[...] + g2_ref[...] * y
    o_ref[...] = _rms(x2, nw_ref[...])


def _combine(out_rows, dest, gates, x1, gate2, normf_w, seq):
    t, d = x1.shape
    tm = min(128, seq)
    tps = seq // tm
    n_tiles = t // tm
    dest3 = dest.reshape(n_tiles, 1, tm * TOP_K)
    return pl.pallas_call(
        functools.partial(_combine_body, tm, n_tiles),
        grid=(n_tiles,),
        in_specs=[pl.BlockSpec((None, 1, tm * TOP_K), lambda i: (i, 0, 0), memory_space=pltpu.SMEM),
                  pl.BlockSpec((None, 1, tm * TOP_K), lambda i: (jnp.minimum(i + 1, n_tiles - 1), 0, 0),
                               memory_space=pltpu.SMEM),
                  pl.BlockSpec(memory_space=pl.ANY),
                  pl.BlockSpec((tm, LANES), lambda i: (i, 0)),
                  pl.BlockSpec((tm, d), lambda i: (i, 0)),
                  pl.BlockSpec((None, 1, d), lambda i: (i // tps, 0, 0)),
                  pl.BlockSpec((1, d), lambda i: (0, 0))],
        out_specs=pl.BlockSpec((tm, d), lambda i: (i, 0)),
        out_shape=jax.ShapeDtypeStruct((t, d), F32),
        scratch_shapes=[pltpu.VMEM((2, TOP_K, tm, d), F32), pltpu.SemaphoreType.DMA((2,))],
        compiler_params=_cp("arbitrary"),
        name="moe_combine",
    )(dest3, dest3, out_rows, gates, x1, gate2, normf_w.reshape(1, d))


def _moe_plan(idx, rank, counts, n_experts):
    bm = MOE_ROWS
    n_assign = idx.shape[0] * TOP_K
    nblk = (counts + bm - 1) // bm
    blk_end = jnp.cumsum(nblk)
    blk_start = blk_end - nblk
    experts = jnp.arange(n_experts, dtype=idx.dtype)
    first_row = jnp.sum(jnp.where(idx[..., None] == experts, bm * blk_start, 0), axis=-1)
    dest = (first_row + rank).astype(jnp.int32)
    nb = (n_assign + n_experts * (bm - 1) + bm - 1) // bm
    n_active = blk_end[-1].astype(jnp.int32)
    gaps = jnp.stack([bm * blk_start + counts, bm * nblk - counts, jnp.full_like(counts, n_active)]).astype(jnp.int32)
    return dest, gaps, blk_start.astype(jnp.int32), nblk.astype(jnp.int32), n_active, nb


def kernel(x, c, positions, w_ada, b_ada, norm1_w, w_in, rwkv_mu_rkvg, rwkv_mu_wa, rwkv_w0, rwkv_w1, rwkv_w2, rwkv_a0, rwkv_a1, rwkv_a2, rwkv_k_k, rwkv_k_a, rwkv_r_k, rwkv_lnx_w, rwkv_lnx_b, w_out, norm2_w, w_router, b_router, w_gate_up, b_gate_up, w_down, b_down, normf_w):
    bsz, seq, d = x.shape
    t = bsz * seq
    depth = w_ada.shape[0]
    rw = rwkv_w0.shape[1]
    ret_w = d - rw
    n_experts = w_router.shape[2]
    x2d = x.reshape(t, d)
    for l in range(depth):
        mod = _adaln(c, w_ada[l], b_ada[l])
        shift1, scale1, gate1, shift2, scale2, gate2 = [m.reshape(bsz, 1, d) for m in jnp.split(mod, 6, axis=-1)]
        h, tw, la = _norm1_lora(x2d, norm1_w[l].reshape(1, d), scale1, shift1,
                                rwkv_mu_wa[l, 0].reshape(1, d), rwkv_mu_wa[l, 1].reshape(1, d),
                                rwkv_w1[l].astype(BF16), rwkv_a1[l].astype(BF16), seq)
        p = _matmul(h, w_in[l].astype(BF16))
        mu = rwkv_mu_rkvg[l].reshape(4, 1, rw)
        prm = (mu[0], mu[1], mu[2], mu[3], rwkv_w0[l].reshape(1, rw), rwkv_a0[l].reshape(1, rw),
               rwkv_k_k[l].reshape(1, rw), rwkv_k_a[l].reshape(1, rw), rwkv_r_k[l].reshape(1, rw),
               rwkv_lnx_w[l].reshape(1, rw), rwkv_lnx_b[l].reshape(1, rw))
        y_rwkv = _rwkv(p, tw, la, prm, rwkv_w2[l].astype(BF16), rwkv_a2[l].astype(BF16), bsz, seq, rw)
        y_ret = _retention(p, positions, bsz, seq, 4 * rw, ret_w)
        wo = w_out[l].astype(BF16)
        x2d = _outproj(y_rwkv, y_ret, wo[:rw], wo[rw:], x2d, gate1, seq)
        h2, idx, gates, rank, counts = _router(x2d, norm2_w[l].reshape(1, d), scale2, shift2,
                                               w_router[l], b_router[l], seq)
        counts = counts[0, :n_experts]
        dest, gaps, blk_start, nblk, n_active, nb = _moe_plan(idx[:, :TOP_K], rank[:, :TOP_K], counts, n_experts)
        rows = _dispatch(h2, dest, gaps, nb * MOE_ROWS)
        de = w_down.shape[2]
        up_tiles = de // min(MOE_UP_TILE, de)
        act = _gate_up(rows, _moe_steps(counts, blk_start, nblk, n_active, up_tiles, nb), w_gate_up[l], b_gate_up[l])
        tn_down = min(MOE_DOWN_TILE, d)
        out_rows = _down(act, _moe_steps(counts, blk_start, nblk, n_active, d // tn_down, nb), w_down[l], b_down[l], tn_down)
        if l + 1 < depth:
            raise NotImplementedError("the final norm is fused into the last layer's combine")
        out = _combine(out_rows, dest, gates, x2d, gate2, normf_w, seq)
    return out.reshape(bsz, seq, d)
```

```python
import functools

import jax
import jax.numpy as jnp
from jax import lax
from jax.experimental import pallas as pl
from jax.experimental.pallas import tpu as pltpu

F32 = jnp.float32
BF16 = jnp.bfloat16

RWKV_HEAD = 64
RET_HEAD = 256
RET_CHUNK = 128
ROPE_BASE = 10000.0
TOP_K = 4
SWIGLU_LIMIT = 7.0
SWIGLU_ALPHA = 1.702
NORM_EPS = 1e-5
RWKV_GN_EPS = 64e-5
RET_GN_EPS = 1e-6

LANES = 128
SUBLANES = 8
RWKV_CHUNK = 64
RWKV_GROUP = 256
HEADS_PER_GROUP = RWKV_GROUP // RWKV_HEAD
MOE_ROWS = 512
MOE_UP_TILE = 256
MOE_DOWN_TILE = 2048
VMEM_LIMIT = 56 << 20


def _cp(*sem):
    return pltpu.CompilerParams(dimension_semantics=sem, vmem_limit_bytes=VMEM_LIMIT)


def _dot(a, b):
    return jnp.dot(a, b, preferred_element_type=F32)


def _dot_nt(a, b):
    return lax.dot_general(a, b, (((1,), (1,)), ((), ())), preferred_element_type=F32)


def _dot_tn(a, b):
    return lax.dot_general(a, b, (((0,), (0,)), ((), ())), preferred_element_type=F32)


def _split2(x):
    hi = x.astype(BF16)
    lo = (x - hi.astype(F32)).astype(BF16)
    return hi, lo


def _rms(x, w):
    return x * lax.rsqrt(jnp.mean(x * x, axis=-1, keepdims=True) + NORM_EPS) * w


def _adaln_body(c_ref, w_ref, b_ref, o_ref):
    c = c_ref[...]
    ca = c * jax.nn.sigmoid(c)
    o_ref[...] = _dot(ca.astype(BF16), w_ref[...].astype(BF16)) + b_ref[...]


def _adaln(c, w_ada, b_ada):
    bsz, d = c.shape
    n = w_ada.shape[1]
    tn = min(512, n)
    c8 = jnp.zeros((8, d), F32).at[:bsz].set(c)
    out = pl.pallas_call(
        _adaln_body,
        grid=(n // tn,),
        in_specs=[pl.BlockSpec((8, d), lambda j: (0, 0)),
                  pl.BlockSpec((d, tn), lambda j: (0, j)),
                  pl.BlockSpec((1, tn), lambda j: (0, j))],
        out_specs=pl.BlockSpec((8, tn), lambda j: (0, j)),
        out_shape=jax.ShapeDtypeStruct((8, n), F32),
        compiler_params=_cp("arbitrary"),
        name="adaln",
    )(c8, w_ada, b_ada.reshape(1, n))
    return out[:bsz]


def _norm1_body(tiles_per_seq, x_ref, nw_ref, sc_ref, sh_ref, muw_ref, mua_ref, w1_ref, a1_ref,
                h_ref, tw_ref, la_ref, carry):
    i = pl.program_id(0)

    @pl.when(i % tiles_per_seq == 0)
    def _():
        carry[...] = jnp.zeros_like(carry)

    x = x_ref[...]
    tm = x.shape[0]
    h = _rms(x, nw_ref[...]) * (1.0 + sc_ref[...]) + sh_ref[...]
    row = lax.broadcasted_iota(jnp.int32, h.shape, 0)
    h_prev = jnp.where(row == 0, carry[0:1, :], pltpu.roll(h, 1, 0))
    carry[0:1, :] = h[tm - 1:tm, :]
    dh = h_prev - h
    xw = h + dh * muw_ref[...]
    xa = h + dh * mua_ref[...]
    tw_ref[...] = jnp.tanh(_dot(xw.astype(BF16), w1_ref[...]))
    la_ref[...] = _dot(xa.astype(BF16), a1_ref[...])
    h_ref[...] = h.astype(BF16)


def _norm1_lora(x2d, nw, scale, shift, mu_w, mu_a, w1b, a1b, seq):
    t, d = x2d.shape
    tm = min(512, seq)
    tps = seq // tm
    r = w1b.shape[1]
    row_d = pl.BlockSpec((1, d), lambda i: (0, 0))
    mod = pl.BlockSpec((None, 1, d), lambda i: (i // tps, 0, 0))
    lora = pl.BlockSpec((d, r), lambda i: (0, 0))
    return pl.pallas_call(
        functools.partial(_norm1_body, tps),
        grid=(t // tm,),
        in_specs=[pl.BlockSpec((tm, d), lambda i: (i, 0)), row_d, mod, mod, row_d, row_d, lora, lora],
        out_specs=[pl.BlockSpec((tm, d), lambda i: (i, 0)),
                   pl.BlockSpec((tm, r), lambda i: (i, 0)),
                   pl.BlockSpec((tm, r), lambda i: (i, 0))],
        out_shape=[jax.ShapeDtypeStruct((t, d), BF16),
                   jax.ShapeDtypeStruct((t, r), F32),
                   jax.ShapeDtypeStruct((t, r), F32)],
        scratch_shapes=[pltpu.VMEM((8, d), F32)],
        compiler_params=_cp("arbitrary"),
        name="norm1_lora",
    )(x2d, nw, scale, shift, mu_w, mu_a, w1b, a1b)


def _matmul_body(a_ref, b_ref, o_ref):
    o_ref[...] = _dot(a_ref[...], b_ref[...])


def _matmul(a, b):
    m, k = a.shape
    n = b.shape[1]
    tm, tn = min(1024, m), min(512, n)
    return pl.pallas_call(
        _matmul_body,
        grid=(m // tm, n // tn),
        in_specs=[pl.BlockSpec((tm, k), lambda i, j: (i, 0)),
                  pl.BlockSpec((k, tn), lambda i, j: (0, j))],
        out_specs=pl.BlockSpec((tm, tn), lambda i, j: (i, j)),
        out_shape=jax.ShapeDtypeStruct((m, n), F32),
        compiler_params=_cp("parallel", "arbitrary"),
        name="in_proj",
    )(a, b)


def _rwkv_body(tb, pr_ref, pk_ref, pv_ref, pg_ref, tw_ref, la_ref,
               mur_ref, muk_ref, muv_ref, mug_ref, w0_ref, a0_ref, kk_ref, ka_ref, rk_ref,
               lnw_ref, lnb_ref, w2_ref, a2_ref,
               bd_ref, mstk_ref, sm_ref, im_ref, tri_ref, eye_ref, lvl_ref,
               y_ref,
               carry, state):
    i = pl.program_id(2)
    c = RWKV_CHUNK

    @pl.when(i == 0)
    def _():
        carry[...] = jnp.zeros_like(carry)
        state[...] = jnp.zeros_like(state)

    row = lax.broadcasted_iota(jnp.int32, (tb, RWKV_GROUP), 0)

    def shift_mix(ref, slot, mu):
        cur = ref[...]
        prev = jnp.where(row == 0, carry[slot:slot + 1, :], pltpu.roll(cur, 1, 0))
        carry[slot:slot + 1, :] = cur[tb - 1:tb, :]
        return cur + (prev - cur) * mu

    bd = bd_ref[...]

    def seg_sum(x):
        hi, lo = _split2(x)
        return _dot(hi, bd) + _dot(lo, bd)

    r = shift_mix(pr_ref, 0, mur_ref[...])
    k = shift_mix(pk_ref, 1, muk_ref[...])
    v = shift_mix(pv_ref, 2, muv_ref[...])
    g = shift_mix(pg_ref, 3, mug_ref[...])

    z = -(w0_ref[...] + _dot(tw_ref[...].astype(BF16), w2_ref[...]))
    softplus = jnp.maximum(z, 0.0) + jnp.log(1.0 + jnp.exp(-jnp.abs(z)))
    lw = -jnp.exp(-softplus - 0.5)
    a_icl = jax.nn.sigmoid(a0_ref[...] + _dot(la_ref[...].astype(BF16), a2_ref[...]))
    kk = k * kk_ref[...]
    kk = kk * lax.rsqrt(jnp.maximum(seg_sum(kk * kk), 1e-24))
    k = k * (1.0 + (a_icl - 1.0) * ka_ref[...])
    a_in = -kk
    b_in = kk * a_icl

    mstk = mstk_ref[...]
    strict = sm_ref[...] > 0.0
    incl = im_ref[...] > 0.0
    same_head = bd > 0
    tri = tri_ref[...]

    def stack(x):
        return jnp.concatenate([x.astype(BF16)] * HEADS_PER_GROUP, axis=0) * mstk

    n_ch = tb // c

    def each(fn, *cols):
        return [fn(*args) for args in zip(*cols)]

    rows = [slice(ci * c, (ci + 1) * c) for ci in range(n_ch)]
    lw_c = [lw[sl] for sl in rows]

    def cumsum(x):
        hi, lo = _split2(x)
        return _dot(tri, hi) + _dot(tri, lo)

    cum = each(cumsum, lw_c)
    cum_end = [x[c - 1:c, :] for x in cum]
    p_inv = each(lambda x: jnp.exp(-x), cum)
    p_end = each(lambda x, e: jnp.exp(e - x), cum, cum_end)
    a_d = each(lambda sl, x, l: (a_in[sl] * jnp.exp(x - l)).astype(BF16), rows, cum, lw_c)
    r_d = each(lambda sl, x: r[sl] * jnp.exp(x), rows, cum)
    ar_b = each(lambda x, y: jnp.concatenate([x, y.astype(BF16)], axis=0), a_d, r_d)
    a_t = each(stack, a_d)
    b_t = each(lambda sl, p: stack(b_in[sl] * p), rows, p_inv)
    k_t = each(lambda sl, p: stack(k[sl] * p), rows, p_inv)
    v_t = each(lambda sl: stack(v[sl]), rows)
    b_h = each(lambda sl, p: (b_in[sl] * p).astype(BF16), rows, p_end)
    uk_r = each(lambda sl, p: (k[sl] * p).astype(BF16), rows, p_end)
    by_b = each(_dot_nt, ar_b, b_t)
    by_k = each(_dot_nt, ar_b, k_t)
    a_ab = each(lambda x: jnp.where(strict, x[0:c], 0.0), by_b)
    a_ak = each(lambda x: jnp.where(strict, x[0:c], 0.0).astype(BF16), by_k)
    a_rb = each(lambda x: jnp.where(incl, x[c:], 0.0).astype(BF16), by_b)
    a_rk = each(lambda x: jnp.where(incl, x[c:], 0.0).astype(BF16), by_k)
    inv = each(lambda x: eye_ref[...] + x * lvl_ref[0], a_ab)
    for lv in range(1, c.bit_length() - 1):
        inv_bd = each(stack, inv)
        w = each(lambda x, t: _dot((x * lvl_ref[lv]).astype(BF16), t), a_ab, inv_bd)
        inv = each(lambda x, ww: x + _dot(x.astype(BF16), stack(ww)), inv, w)
    invb = each(lambda x: x.astype(BF16), inv)
    a_p = each(_dot, invb, a_t)
    akv = each(_dot, a_ak, v_t)
    u_loc = each(lambda t, x: _dot(t, stack(x)), invb, akv)
    r_p = each(lambda x, m, y: (x + _dot(m, stack(y))).astype(BF16), r_d, a_rb, a_p)
    y_loc = each(lambda m, x, n, y: _dot(m, stack(x)) + _dot(n, y), a_rb, u_loc, a_rk, v_t)
    phi = each(lambda x, y: jnp.where(same_head, _dot_tn(x.astype(BF16), y), 0.0).astype(BF16), a_p, b_h)
    s_loc = each(lambda x, y, sl, z: jnp.where(
        same_head,
        _dot_tn(jnp.concatenate([x.astype(BF16), v[sl].astype(BF16)], axis=0), jnp.concatenate([y, z], axis=0)),
        0.0), u_loc, b_h, rows, uk_r)

    s = state[...]
    ys = []
    for ci in range(n_ch):
        sb = s.astype(BF16)
        ys.append(_dot_nt(r_p[ci], sb) + y_loc[ci])
        s = s * jnp.exp(cum_end[ci]) + _dot(sb, phi[ci]) + s_loc[ci]
    state[...] = s

    y = jnp.concatenate(ys, axis=0)
    inv_n = 1.0 / RWKV_HEAD
    mean = seg_sum(y) * inv_n
    dlt = y - mean
    var = seg_sum(dlt * dlt) * inv_n
    yn = dlt * lax.rsqrt(var + RWKV_GN_EPS) * lnw_ref[...] + lnb_ref[...]
    bonus = seg_sum(r * k * rk_ref[...]) * v
    y_ref[...] = ((yn + bonus) * jax.nn.sigmoid(g)).astype(BF16)


def _rwkv_consts():
    n = RWKV_GROUP
    c = RWKV_CHUNK
    lane = jnp.arange(n)
    head = lane // RWKV_HEAD
    blk = lane // c
    bd = (head[:, None] == head[None, :]).astype(BF16)
    mstk = (blk[:, None] == head[None, :]).astype(BF16)
    t_idx = jnp.arange(c)[:, None]
    s_idx = (lane % c)[None, :]
    strict = (s_idx < t_idx).astype(F32)
    incl = (s_idx <= t_idx).astype(F32)
    eye = (s_idx == t_idx).astype(F32)
    tri = (jnp.arange(c)[None, :] <= jnp.arange(c)[:, None]).astype(BF16)
    lvls = []
    for lv in range(c.bit_length() - 1):
        m = 1 << lv
        hi_lo = ((t_idx // m) % 2 == 1) & ((s_idx // m) % 2 == 0)
        lvls.append((hi_lo & (t_idx // (2 * m) == s_idx // (2 * m))).astype(F32))
    return bd, mstk, strict, incl, tri, eye, jnp.stack(lvls)


def _rwkv(p, tw, la, prm, w2b, a2b, bsz, seq, rw):
    t = p.shape[0]
    g = rw // RWKV_GROUP
    tb = min(1024, seq)
    nt = seq // tb
    r = tw.shape[1]
    bd, mstk, strict, incl, tri, eye, lvls = _rwkv_consts()

    def pcol(m):
        return pl.BlockSpec((tb, RWKV_GROUP), lambda b, gg, i: (b * nt + i, m * g + gg))

    lor = pl.BlockSpec((tb, r), lambda b, gg, i: (b * nt + i, 0))
    chan = pl.BlockSpec((1, RWKV_GROUP), lambda b, gg, i: (0, gg))
    lw2 = pl.BlockSpec((r, RWKV_GROUP), lambda b, gg, i: (0, gg))
    sq = pl.BlockSpec((RWKV_GROUP, RWKV_GROUP), lambda b, gg, i: (0, 0))
    tri_spec = pl.BlockSpec((RWKV_CHUNK, RWKV_CHUNK), lambda b, gg, i: (0, 0))
    wide = pl.BlockSpec((RWKV_CHUNK, RWKV_GROUP), lambda b, gg, i: (0, 0))
    return pl.pallas_call(
        functools.partial(_rwkv_body, tb),
        grid=(bsz, g, nt),
        in_specs=[pcol(0), pcol(1), pcol(2), pcol(3), lor, lor] + [chan] * 11 + [lw2, lw2]
                 + [sq, sq, wide, wide, tri_spec, wide, pl.BlockSpec(lvls.shape, lambda b, gg, i: (0, 0, 0))],
        out_specs=pl.BlockSpec((tb, RWKV_GROUP), lambda b, gg, i: (b * nt + i, gg)),
        out_shape=jax.ShapeDtypeStruct((t, rw), BF16),
        scratch_shapes=[pltpu.VMEM((8, RWKV_GROUP), F32), pltpu.VMEM((RWKV_GROUP, RWKV_GROUP), F32)],
        compiler_params=_cp("parallel", "parallel", "arbitrary"),
        name="rwkv7",
    )(p, p, p, p, tw, la, *prm, w2b, a2b, bd, mstk, strict, incl, tri, eye, lvls)


def _ret_body(heads, q_ref, k_ref, v_ref, g_ref, pos_ref, invf_ref, inner_ref, qd_ref, kd_ref, cd_ref,
              o_ref, state):
    @pl.when(pl.program_id(1) == 0)
    def _():
        state[...] = jnp.zeros_like(state)

    half = RET_HEAD // 2
    ang = pos_ref[...].astype(F32) * invf_ref[...]
    cos = jnp.cos(ang)
    sin = jnp.sin(ang)

    def rope(x):
        x1, x2 = x[:, :half], x[:, half:]
        return jnp.concatenate([x1 * cos - x2 * sin, x1 * sin + x2 * cos], axis=-1)

    for hh in range(heads):
        sl = slice(hh * RET_HEAD, (hh + 1) * RET_HEAD)
        q = rope(q_ref[:, sl])
        k = rope(k_ref[:, sl]) * (RET_HEAD ** -0.5)
        v = v_ref[:, sl].astype(BF16)
        g = g_ref[:, sl]
        qb = q.astype(BF16)
        scores = _dot_nt(qb, k.astype(BF16)) * inner_ref[hh]
        st = state[hh]
        o = _dot(scores.astype(BF16), v) + _dot(qb, st.astype(BF16)) * qd_ref[hh]
        state[hh] = st * cd_ref[hh] + _dot_tn((k * kd_ref[hh]).astype(BF16), v)
        on = o * lax.rsqrt(jnp.mean(o * o, axis=-1, keepdims=True) + RET_GN_EPS)
        o_ref[:, sl] = (on * (g * jax.nn.sigmoid(g))).astype(BF16)


def _retention(p, positions, bsz, seq, col0, width):
    t = p.shape[0]
    heads = width // RET_HEAD
    c = RET_CHUNK
    nc = seq // c
    half = RET_HEAD // 2
    cb = col0 // width
    inv_freq = (ROPE_BASE ** (-jnp.arange(half, dtype=F32) / half)).reshape(1, half)
    log_gamma = jnp.log(1.0 - 2.0 ** (-5.0 - jnp.arange(heads, dtype=F32)))
    idx = jnp.arange(c, dtype=F32)
    diff = idx[:, None] - idx[None, :]
    inner = jnp.where(diff >= 0, jnp.exp(log_gamma[:, None, None] * jnp.maximum(diff, 0.0)), 0.0)
    q_decay = jnp.exp(log_gamma[:, None] * (idx + 1.0))[:, :, None]
    k_decay = jnp.exp(log_gamma[:, None] * (c - 1.0 - idx))[:, :, None]
    chunk_decay = jnp.broadcast_to(jnp.exp(log_gamma * c)[:, None, None], (heads, 1, RET_HEAD))

    def pcol(m):
        return pl.BlockSpec((c, width), lambda b, i: (b * nc + i, cb + m))

    def full(a):
        return pl.BlockSpec(a.shape, lambda b, i: (0,) * a.ndim)

    consts = (inv_freq, inner, q_decay, k_decay, chunk_decay)
    return pl.pallas_call(
        functools.partial(_ret_body, heads),
        grid=(bsz, nc),
        in_specs=[pcol(0), pcol(1), pcol(2), pcol(3),
                  pl.BlockSpec((c, 1), lambda b, i: (b * nc + i, 0))] + [full(a) for a in consts],
        out_specs=pl.BlockSpec((c, width), lambda b, i: (b * nc + i, 0)),
        out_shape=jax.ShapeDtypeStruct((t, width), BF16),
        scratch_shapes=[pltpu.VMEM((heads, RET_HEAD, RET_HEAD), F32)],
        compiler_params=_cp("parallel", "arbitrary"),
        name="retention",
    )(p, p, p, p, positions.reshape(t, 1), *consts)


def _outproj_body(ya_ref, yb_ref, wa_ref, wb_ref, x_ref, g_ref, o_ref):
    mix = _dot(ya_ref[...], wa_ref[...]) + _dot(yb_ref[...], wb_ref[...])
    o_ref[...] = x_ref[...] + g_ref[...] * mix


def _outproj(ya, yb, wa, wb, x2d, gate, seq):
    t, d = x2d.shape
    ka, kb = ya.shape[1], yb.shape[1]
    tm, tn = min(1024, seq), min(512, d)
    tps = seq // tm
    return pl.pallas_call(
        _outproj_body,
        grid=(t // tm, d // tn),
        in_specs=[pl.BlockSpec((tm, ka), lambda i, j: (i, 0)),
                  pl.BlockSpec((tm, kb), lambda i, j: (i, 0)),
                  pl.BlockSpec((ka, tn), lambda i, j: (0, j)),
                  pl.BlockSpec((kb, tn), lambda i, j: (0, j)),
                  pl.BlockSpec((tm, tn), lambda i, j: (i, j)),
                  pl.BlockSpec((None, 1, tn), lambda i, j: (i // tps, 0, j))],
        out_specs=pl.BlockSpec((tm, tn), lambda i, j: (i, j)),
        out_shape=jax.ShapeDtypeStruct((t, d), F32),
        compiler_params=_cp("parallel", "arbitrary"),
        name="out_proj",
    )(ya, yb, wa, wb, x2d, gate)


def _pack_bf16_pairs(h):
    half = h.shape[1] // 2
    hi = lax.bitcast_convert_type(h[:, :half].astype(BF16).astype(F32), jnp.int32)
    lo = lax.bitcast_convert_type(h[:, half:].astype(BF16).astype(F32), jnp.int32)
    return hi | lax.shift_right_logical(lo, 16)


def _unpack_bf16_pairs(w):
    hi = lax.bitcast_convert_type(w & -65536, F32).astype(BF16)
    lo = lax.bitcast_convert_type(w << 16, F32).astype(BF16)
    return hi, lo


def _router_body(x_ref, nw_ref, sc_ref, sh_ref, wrh_ref, wrl_ref, br_ref, tri_ref,
                 h_ref, idx_ref, gate_ref, rank_ref, cnt_ref, carry):
    @pl.when(pl.program_id(0) == 0)
    def _():
        carry[...] = jnp.zeros_like(carry)

    h = _rms(x_ref[...], nw_ref[...]) * (1.0 + sc_ref[...]) + sh_ref[...]
    h_ref[...] = _pack_bf16_pairs(h)
    h_hi, h_lo = _split2(h)
    logits = _dot(h_hi, wrh_ref[...]) + _dot(h_hi, wrl_ref[...]) + _dot(h_lo, wrh_ref[...]) + br_ref[...]
    lane = lax.broadcasted_iota(jnp.int32, logits.shape, 1)
    vals, idxs = [], []
    rest = logits
    for _ in range(TOP_K):
        m = jnp.max(rest, axis=-1, keepdims=True)
        ix = jnp.min(jnp.where(rest == m, lane, LANES), axis=-1, keepdims=True)
        vals.append(m)
        idxs.append(ix)
        rest = jnp.where(lane == ix, -jnp.inf, rest)
    exps = [jnp.exp(vv - vals[0]) for vv in vals]
    den = exps[0]
    for e in exps[1:]:
        den = den + e
    hot = (lane == idxs[0]).astype(F32)
    for ix in idxs[1:]:
        hot = hot + (lane == ix).astype(F32)
    before = _dot(tri_ref[...], hot.astype(BF16)) + carry[0:1, :]
    carry[0:1, :] = carry[0:1, :] + jnp.sum(hot, axis=0, keepdims=True)
    idx_out = jnp.zeros(logits.shape, jnp.int32)
    gate_out = jnp.zeros(logits.shape, F32)
    rank_out = jnp.zeros(logits.shape, F32)
    for kk in range(TOP_K):
        rk = jnp.sum(jnp.where(lane == idxs[kk], before, 0.0), axis=-1, keepdims=True)
        idx_out = jnp.where(lane == kk, idxs[kk], idx_out)
        gate_out = jnp.where(lane == kk, exps[kk] / den, gate_out)
        rank_out = jnp.where(lane == kk, rk, rank_out)
    idx_ref[...] = idx_out
    gate_ref[...] = gate_out
    rank_ref[...] = rank_out.astype(jnp.int32)
    cnt_ref[...] = jnp.broadcast_to(carry[0:1, :], cnt_ref.shape).astype(jnp.int32)


def _router(x2d, nw, scale, shift, w_router, b_router, seq):
    t, d = x2d.shape
    e = w_router.shape[1]
    tm = min(512, seq)
    tps = seq // tm
    wr = jnp.zeros((d, LANES), F32).at[:, :e].set(w_router)
    wr_hi, wr_lo = _split2(wr)
    br = jnp.full((1, LANES), -jnp.inf, F32).at[0, :e].set(b_router)
    tri = (jnp.arange(tm)[None, :] < jnp.arange(tm)[:, None]).astype(BF16)
    row_d = pl.BlockSpec((1, d), lambda i: (0, 0))
    mod = pl.BlockSpec((None, 1, d), lambda i: (i // tps, 0, 0))
    lane_out = pl.BlockSpec((tm, LANES), lambda i: (i, 0))
    return pl.pallas_call(
        _router_body,
        grid=(t // tm,),
        in_specs=[pl.BlockSpec((tm, d), lambda i: (i, 0)), row_d, mod, mod,
                  pl.BlockSpec((d, LANES), lambda i: (0, 0)),
                  pl.BlockSpec((d, LANES), lambda i: (0, 0)),
                  pl.BlockSpec((1, LANES), lambda i: (0, 0)),
                  pl.BlockSpec((tm, tm), lambda i: (0, 0))],
        out_specs=[pl.BlockSpec((tm, d // 2), lambda i: (i, 0)), lane_out, lane_out, lane_out,
                   pl.BlockSpec((8, LANES), lambda i: (0, 0))],
        out_shape=[jax.ShapeDtypeStruct((t, d // 2), jnp.int32),
                   jax.ShapeDtypeStruct((t, LANES), jnp.int32),
                   jax.ShapeDtypeStruct((t, LANES), F32),
                   jax.ShapeDtypeStruct((t, LANES), jnp.int32),
                   jax.ShapeDtypeStruct((8, LANES), jnp.int32)],
        scratch_shapes=[pltpu.VMEM((8, LANES), F32)],
        compiler_params=_cp("arbitrary"),
        name="norm2_router",
    )(x2d, nw, scale, shift, wr_hi, wr_lo, br, tri)


def _row_copy(src, dst, sem):
    return pltpu.make_async_copy(src, dst, sem)


def _dispatch_body(tm, gaps_ref, dest_ref, h_ref, rows_out, zeros, sem, zsem):
    zr, half = zeros.shape
    n_experts = gaps_ref.shape[1]

    def zero_fill(start_or_wait):
        def gap(e, carry_):
            pos = gaps_ref[0, e]
            length = gaps_ref[1, e]
            head = jnp.minimum((-pos) % SUBLANES, length)
            n_tiles = (length - head) // SUBLANES
            body = pos + head
            tail = body + n_tiles * SUBLANES

            def row(at):
                def one(r, c2):
                    start_or_wait(pltpu.make_async_copy(zeros.at[pl.ds(0, 1)], rows_out.at[pl.ds(at + r, 1)], zsem))
                    return c2
                return one

            def tile(r, c2):
                dst = rows_out.at[pl.ds(pl.multiple_of(body + r * SUBLANES, SUBLANES), SUBLANES)]
                start_or_wait(pltpu.make_async_copy(zeros.at[pl.ds(0, SUBLANES)], dst, zsem))
                return c2

            lax.fori_loop(0, head, row(pos), 0)
            lax.fori_loop(0, n_tiles, tile, 0)
            lax.fori_loop(0, length - head - n_tiles * SUBLANES, row(tail), 0)
            return carry_

        lax.fori_loop(0, n_experts, gap, 0)

        def spare(b, carry_):
            for hh in range(MOE_ROWS // zr):
                dst = rows_out.at[pl.ds(pl.multiple_of(b * MOE_ROWS + hh * zr, zr), zr)]
                start_or_wait(pltpu.make_async_copy(zeros, dst, zsem))
            return carry_

        lax.fori_loop(gaps_ref[2, 0], rows_out.shape[0] // MOE_ROWS, spare, 0)

    @pl.when(pl.program_id(0) == 0)
    def _():
        zeros[...] = jnp.zeros_like(zeros)
        zero_fill(lambda cp: cp.start())

    def issue(r, carry_):
        for kk in range(TOP_K):
            d = dest_ref[0, r * TOP_K + kk]
            _row_copy(h_ref.at[pl.ds(r, 1)], rows_out.at[pl.ds(d, 1)], sem).start(priority=kk % 2)
        return carry_

    def drain(r, carry_):
        for _ in range(TOP_K):
            _row_copy(h_ref.at[pl.ds(0, 1)], rows_out.at[pl.ds(0, 1)], sem).wait()
        return carry_

    lax.fori_loop(0, tm, issue, 0)
    lax.fori_loop(0, tm, drain, 0)

    @pl.when(pl.program_id(0) == 0)
    def _():
        zero_fill(lambda cp: cp.wait())


def _dispatch(h2, dest, gaps, cap):
    t, d = h2.shape
    tm = min(128, t)
    dest3 = dest.reshape(t // tm, 1, tm * TOP_K)
    return pl.pallas_call(
        functools.partial(_dispatch_body, tm),
        grid=(t // tm,),
        in_specs=[pl.BlockSpec(memory_space=pltpu.SMEM),
                  pl.BlockSpec((None, 1, tm * TOP_K), lambda i: (i, 0, 0), memory_space=pltpu.SMEM),
                  pl.BlockSpec((tm, d), lambda i: (i, 0))],
        out_specs=pl.BlockSpec(memory_space=pl.ANY),
        out_shape=jax.ShapeDtypeStruct((cap, d), h2.dtype),
        scratch_shapes=[pltpu.VMEM((MOE_ROWS // 2, d), h2.dtype), pltpu.SemaphoreType.DMA, pltpu.SemaphoreType.DMA],
        compiler_params=_cp("arbitrary"),
        name="moe_dispatch",
    )(gaps, dest3, h2)


def _moe_steps(counts, blk_start, nblk, n_active, n_tiles, nb):
    n_experts = blk_start.shape[0]
    total = n_tiles * nb
    s = jnp.arange(total, dtype=jnp.int32)
    step_end = n_tiles * (blk_start + nblk)
    e = jnp.minimum(jnp.sum(step_end[None, :] <= s[:, None], axis=1), n_experts - 1).astype(jnp.int32)
    experts = jnp.arange(n_experts, dtype=jnp.int32)
    hot = e[:, None] == experts[None, :]

    def pick(table):
        return jnp.sum(jnp.where(hot, table[None, :].astype(jnp.int32), 0), axis=1)

    e_blk0, e_nblk = pick(blk_start), pick(nblk)
    local = s - n_tiles * e_blk0
    per = jnp.maximum(e_nblk, 1)
    tile = local // per
    in_tile = local % per
    rb = e_blk0 + in_tile
    first = in_tile == 0
    n_steps = (n_tiles * n_active).astype(jnp.int32)
    active = s < n_steps
    spare = s - n_steps
    slot = (jnp.cumsum(first.astype(jnp.int32)) - 1) % 2
    later = (experts[None, :] > experts[:, None]) & (nblk[None, :] > 0)
    e_after = pick(jnp.min(jnp.where(later, experts[None, :], n_experts), axis=1))
    same = tile + 1 < n_tiles
    e_last = jnp.max(jnp.where(nblk > 0, experts, 0))
    cols = (jnp.where(active, e, e_last), jnp.where(active, tile, n_tiles - 1), jnp.where(active, rb, n_active - 1),
            jnp.where(active, tile, spare % n_tiles), jnp.where(active, rb, n_active + spare // n_tiles),
            first, slot, jnp.where(same, e, jnp.minimum(e_after, n_experts - 1)), jnp.where(same, tile + 1, 0),
            same | (e_after < n_experts),
            pick(counts) - MOE_ROWS * in_tile <= MOE_ROWS // 2)
    return tuple(x.astype(jnp.int32) for x in cols) + (n_steps.reshape(1),)


N_STEP_ARRAYS = 12


def _weight_stream(s, first, slot, e_cur, t_cur, e_nxt, t_nxt, has_nxt, copies, convert):
    @pl.when(first[s] == 1)
    def _():
        sl = slot[s]

        @pl.when(s == 0)
        def _():
            for cp in copies(e_cur[0], t_cur[0], 0):
                cp.start()

        for cp in copies(e_cur[s], t_cur[s], sl):
            cp.wait()
        convert(sl)

        @pl.when(has_nxt[s] == 1)
        def _():
            for cp in copies(e_nxt[s], t_nxt[s], 1 - sl):
                cp.start()


def _gate_up_body(se, st, sri, sto, sro, first, slot, en, tn_, hn, half_blk, n_steps, x_ref, w_hbm, bg_ref, bu_ref,
                  act_ref, wf, wgb, wub, sem):
    s = pl.program_id(0)
    d, tn = wgb.shape
    nj = w_hbm.shape[2] // (2 * tn)
    bm = x_ref.shape[0]

    @pl.when(s >= n_steps[0])
    def _():
        act_ref[...] = jnp.zeros_like(act_ref)

    @pl.when(s < n_steps[0])
    def _():
        def copies(e, j, sl):
            return [pltpu.make_async_copy(w_hbm.at[e, :, pl.ds(pl.multiple_of((m * nj + j) * tn, tn), tn)],
                                          wf.at[sl, m], sem.at[sl]) for m in range(2)]

        def convert(sl):
            wgb[...] = wf[sl, 0].astype(BF16)
            wub[...] = wf[sl, 1].astype(BF16)

        _weight_stream(s, first, slot, se, st, en, tn_, hn, copies, convert)
        half = d // 2

        def rows(n):
            xa, xb = _unpack_bf16_pairs(x_ref[0:n, :])

            def proj(w, b_ref):
                return _dot(xa, w[:half, :]) + _dot(xb, w[half:, :]) + b_ref[...]

            gate = jnp.minimum(proj(wgb, bg_ref), SWIGLU_LIMIT)
            up = jnp.clip(proj(wub, bu_ref), -SWIGLU_LIMIT, SWIGLU_LIMIT)
            act_ref[0:n, :] = ((up + 1.0) * gate * jax.nn.sigmoid(SWIGLU_ALPHA * gate)).astype(BF16)

        @pl.when(half_blk[s] == 0)
        def _():
            rows(bm)

        @pl.when(half_blk[s] == 1)
        def _():
            rows(bm // 2)
            act_ref[bm // 2:, :] = jnp.zeros((bm - bm // 2, tn), BF16)


def _gate_up(rows, steps, wgu, bgu):
    cap, half = rows.shape
    d = 2 * half
    e, _, two_de = wgu.shape
    de = two_de // 2
    bm = MOE_ROWS
    tn = min(MOE_UP_TILE, de)
    nj = de // tn
    grid_spec = pltpu.PrefetchScalarGridSpec(
        num_scalar_prefetch=N_STEP_ARRAYS,
        grid=(nj * (cap // bm),),
        in_specs=[pl.BlockSpec((bm, half), lambda s, se, st, sri, *_: (sri[s], 0)),
                  pl.BlockSpec(memory_space=pl.ANY),
                  pl.BlockSpec((None, 1, tn), lambda s, se, st, *_: (se[s], 0, st[s])),
                  pl.BlockSpec((None, 1, tn), lambda s, se, st, *_: (se[s], 0, nj + st[s]))],
        out_specs=pl.BlockSpec((bm, tn), lambda s, se, st, sri, sto, sro, *_: (sro[s], sto[s])),
        scratch_shapes=[pltpu.VMEM((2, 2, d, tn), F32), pltpu.VMEM((d, tn), BF16), pltpu.VMEM((d, tn), BF16),
                        pltpu.SemaphoreType.DMA((2,))],
    )
    bgu3 = bgu.reshape(e, 1, two_de)
    return pl.pallas_call(
        _gate_up_body,
        grid_spec=grid_spec,
        out_shape=jax.ShapeDtypeStruct((cap, de), BF16),
        compiler_params=_cp("arbitrary"),
        name="moe_gate_up",
    )(*steps, rows, wgu, bgu3, bgu3)


def _down_body(se, st, sri, sto, sro, first, slot, en, tn_, hn, half_blk, n_steps, a_ref, w_hbm, bd_ref, o_ref,
               wf, wdb, sem):
    s = pl.program_id(0)
    tn = wdb.shape[1]
    bm = a_ref.shape[0]

    @pl.when(s >= n_steps[0])
    def _():
        o_ref[...] = jnp.zeros_like(o_ref)

    @pl.when(s < n_steps[0])
    def _():
        def copies(e, j, sl):
            return [pltpu.make_async_copy(w_hbm.at[e, :, pl.ds(pl.multiple_of(j * tn, tn), tn)],
                                          wf.at[sl], sem.at[sl])]

        def convert(sl):
            wdb[...] = wf[sl].astype(BF16)

        _weight_stream(s, first, slot, se, st, en, tn_, hn, copies, convert)

        @pl.when(half_blk[s] == 0)
        def _():
            o_ref[...] = _dot(a_ref[...], wdb[...]) + bd_ref[...]

        @pl.when(half_blk[s] == 1)
        def _():
            o_ref[0:bm // 2, :] = _dot(a_ref[0:bm // 2, :], wdb[...]) + bd_ref[...]
            o_ref[bm // 2:, :] = jnp.zeros((bm - bm // 2, tn), F32)


def _down(act, steps, wd, bd, tn):
    cap, de = act.shape
    e, _, d = wd.shape
    bm = MOE_ROWS
    grid_spec = pltpu.PrefetchScalarGridSpec(
        num_scalar_prefetch=N_STEP_ARRAYS,
        grid=((d // tn) * (cap // bm),),
        in_specs=[pl.BlockSpec((bm, de), lambda s, se, st, sri, *_: (sri[s], 0)),
                  pl.BlockSpec(memory_space=pl.ANY),
                  pl.BlockSpec((None, 1, tn), lambda s, se, st, *_: (se[s], 0, st[s]))],
        out_specs=pl.BlockSpec((bm, tn), lambda s, se, st, sri, sto, sro, *_: (sro[s], sto[s])),
        scratch_shapes=[pltpu.VMEM((2, de, tn), F32), pltpu.VMEM((de, tn), BF16), pltpu.SemaphoreType.DMA((2,))],
    )
    return pl.pallas_call(
        _down_body,
        grid_spec=grid_spec,
        out_shape=jax.ShapeDtypeStruct((cap, d), F32),
        compiler_params=_cp("arbitrary"),
        name="moe_down",
    )(*steps, act, wd, bd.reshape(e, 1, d))


def _combine_body(tm, n_tiles, dest_ref, dest_next_ref, rows_ref, gates_ref, x_ref, g2_ref, nw_ref, o_ref,
                  buf, sem):
    i = pl.program_id(0)
    slot = i % 2

    def fetch(dref, sl):
        def issue(r, carry_):
            for kk in range(TOP_K):
                d = dref[0, r * TOP_K + kk]
                _row_copy(rows_ref.at[pl.ds(d, 1)], buf.at[sl, kk, pl.ds(r, 1)], sem.at[sl]).start(priority=kk % 2)
            return carry_

        lax.fori_loop(0, tm, issue, 0)

    @pl.when(i == 0)
    def _():
        fetch(dest_ref, 0)

    @pl.when(i + 1 < n_tiles)
    def _():
        fetch(dest_next_ref, 1 - slot)

    def drain(r, carry_):
        for kk in range(TOP_K):
            _row_copy(rows_ref.at[pl.ds(0, 1)], buf.at[slot, kk, pl.ds(0, 1)], sem.at[slot]).wait()
        return carry_

    lax.fori_loop(0, tm, drain, 0)
    gates = gates_ref[...]
    y = gates[:, 0:1] * buf[slot, 0]
    for kk in range(1, TOP_K):
        y = y + gates[:, kk:kk + 1] * buf[slot, kk]
    x2 = x_ref[...] + g2_ref[...] * y
    o_ref[...] = _rms(x2, nw_ref[...])


def _combine(out_rows, dest, gates, x1, gate2, normf_w, seq):
    t, d = x1.shape
    tm = min(128, seq)
    tps = seq // tm
    n_tiles = t // tm
    dest3 = dest.reshape(n_tiles, 1, tm * TOP_K)
    return pl.pallas_call(
        functools.partial(_combine_body, tm, n_tiles),
        grid=(n_tiles,),
        in_specs=[pl.BlockSpec((None, 1, tm * TOP_K), lambda i: (i, 0, 0), memory_space=pltpu.SMEM),
                  pl.BlockSpec((None, 1, tm * TOP_K), lambda i: (jnp.minimum(i + 1, n_tiles - 1), 0, 0),
                               memory_space=pltpu.SMEM),
                  pl.BlockSpec(memory_space=pl.ANY),
                  pl.BlockSpec((tm, LANES), lambda i: (i, 0)),
                  pl.BlockSpec((tm, d), lambda i: (i, 0)),
                  pl.BlockSpec((None, 1, d), lambda i: (i // tps, 0, 0)),
                  pl.BlockSpec((1, d), lambda i: (0, 0))],
        out_specs=pl.BlockSpec((tm, d), lambda i: (i, 0)),
        out_shape=jax.ShapeDtypeStruct((t, d), F32),
        scratch_shapes=[pltpu.VMEM((2, TOP_K, tm, d), F32), pltpu.SemaphoreType.DMA((2,))],
        compiler_params=_cp("arbitrary"),
        name="moe_combine",
    )(dest3, dest3, out_rows, gates, x1, gate2, normf_w.reshape(1, d))


def _moe_plan(idx, rank, counts, n_experts):
    bm = MOE_ROWS
    n_assign = idx.shape[0] * TOP_K
    nblk = (counts + bm - 1) // bm
    blk_end = jnp.cumsum(nblk)
    blk_start = blk_end - nblk
    experts = jnp.arange(n_experts, dtype=idx.dtype)
    first_row = jnp.sum(jnp.where(idx[..., None] == experts, bm * blk_start, 0), axis=-1)
    dest = (first_row + rank).astype(jnp.int32)
    nb = (n_assign + n_experts * (bm - 1) + bm - 1) // bm
    n_active = blk_end[-1].astype(jnp.int32)
    gaps = jnp.stack([bm * blk_start + counts, bm * nblk - counts, jnp.full_like(counts, n_active)]).astype(jnp.int32)
    return dest, gaps, blk_start.astype(jnp.int32), nblk.astype(jnp.int32), n_active, nb


def kernel(x, c, positions, w_ada, b_ada, norm1_w, w_in, rwkv_mu_rkvg, rwkv_mu_wa, rwkv_w0, rwkv_w1, rwkv_w2, rwkv_a0, rwkv_a1, rwkv_a2, rwkv_k_k, rwkv_k_a, rwkv_r_k, rwkv_lnx_w, rwkv_lnx_b, w_out, norm2_w, w_router, b_router, w_gate_up, b_gate_up, w_down, b_down, normf_w):
    bsz, seq, d = x.shape
    t = bsz * seq
    depth = w_ada.shape[0]
    rw = rwkv_w0.shape[1]
    ret_w = d - rw
    n_experts = w_router.shape[2]
    x2d = x.reshape(t, d)
    for l in range(depth):
        mod = _adaln(c, w_ada[l], b_ada[l])
        shift1, scale1, gate1, shift2, scale2, gate2 = [m.reshape(bsz, 1, d) for m in jnp.split(mod, 6, axis=-1)]
        h, tw, la = _norm1_lora(x2d, norm1_w[l].reshape(1, d), scale1, shift1,
                                rwkv_mu_wa[l, 0].reshape(1, d), rwkv_mu_wa[l, 1].reshape(1, d),
                                rwkv_w1[l].astype(BF16), rwkv_a1[l].astype(BF16), seq)
        p = _matmul(h, w_in[l].astype(BF16))
        mu = rwkv_mu_rkvg[l].reshape(4, 1, rw)
        prm = (mu[0], mu[1], mu[2], mu[3], rwkv_w0[l].reshape(1, rw), rwkv_a0[l].reshape(1, rw),
               rwkv_k_k[l].reshape(1, rw), rwkv_k_a[l].reshape(1, rw), rwkv_r_k[l].reshape(1, rw),
               rwkv_lnx_w[l].reshape(1, rw), rwkv_lnx_b[l].reshape(1, rw))
        y_rwkv = _rwkv(p, tw, la, prm, rwkv_w2[l].astype(BF16), rwkv_a2[l].astype(BF16), bsz, seq, rw)
        y_ret = _retention(p, positions, bsz, seq, 4 * rw, ret_w)
        wo = w_out[l].astype(BF16)
        x2d = _outproj(y_rwkv, y_ret, wo[:rw], wo[rw:], x2d, gate1, seq)
        h2, idx, gates, rank, counts = _router(x2d, norm2_w[l].reshape(1, d), scale2, shift2,
                                               w_router[l], b_router[l], seq)
        counts = counts[0, :n_experts]
        dest, gaps, blk_start, nblk, n_active, nb = _moe_plan(idx[:, :TOP_K], rank[:, :TOP_K], counts, n_experts)
        rows = _dispatch(h2, dest, gaps, nb * MOE_ROWS)
        de = w_down.shape[2]
        up_tiles = de // min(MOE_UP_TILE, de)
        act = _gate_up(rows, _moe_steps(counts, blk_start, nblk, n_active, up_tiles, nb), w_gate_up[l], b_gate_up[l])
        tn_down = min(MOE_DOWN_TILE, d)
        out_rows = _down(act, _moe_steps(counts, blk_start, nblk, n_active, d // tn_down, nb), w_down[l], b_down[l], tn_down)
        if l + 1 < depth:
            raise NotImplementedError("the final norm is fused into the last layer's combine")
        out = _combine(out_rows, dest, gates, x2d, gate2, normf_w, seq)
    return out.reshape(bsz, seq, d)
```
